```python
import jax, jax.numpy as jnp
from jax import lax
import numpy as np

D_MODEL = 4096
BATCH = 4
SEQ = 4096
DEPTH = 1
DEC_BATCH = 1
DEC_SEQ = 16384
PAST_LEN = 128

HEAD_DIM = 128
D_MIX = D_MODEL
ATTN_WIDTH = D_MIX // 2
N_Q_HEADS = ATTN_WIDTH // HEAD_DIM
N_KV_HEADS = N_Q_HEADS // 4
KV_GROUP = N_Q_HEADS // N_KV_HEADS
KV_WIDTH = N_KV_HEADS * HEAD_DIM
CONV_WIDTH = D_MIX - ATTN_WIDTH
CONV_GROUPS = CONV_WIDTH // HEAD_DIM
CONV_K = 3
IN_PROJ_WIDTH = ATTN_WIDTH + 2 * KV_WIDTH + 3 * CONV_WIDTH
D_FF = 11008
GRID_W = 64
ROPE_AXIS_DIM = HEAD_DIM // 2
ROPE_THETA = 10000.0
Q_BLOCK = 128
NORM_EPS = 1e-6
FFN_RES_SCALE = 0.5

kernel_name = 'hymba_macaron_gqa_shortconv_encoder'


def rms_norm(x, g):
    xf = x.astype(jnp.float32)
    y = xf * lax.rsqrt(jnp.mean(xf * xf, axis=-1, keepdims=True) + NORM_EPS)
    return (y * g.astype(jnp.float32)).astype(x.dtype)


def swiglu(x, w_gate, w_up, w_down):
    return (jax.nn.silu(x @ w_gate) * (x @ w_up)) @ w_down


def axial_rope_angles(seq_len):
    rows = seq_len // GRID_W
    inv = 1.0 / (ROPE_THETA ** (jnp.arange(0, ROPE_AXIS_DIM, 2, dtype=jnp.float32) / ROPE_AXIS_DIM))
    row_ang = jnp.arange(rows, dtype=jnp.float32)[:, None] * inv
    col_ang = jnp.arange(GRID_W, dtype=jnp.float32)[:, None] * inv
    ang = jnp.concatenate([
        jnp.broadcast_to(row_ang[:, None, :], (rows, GRID_W, inv.shape[0])),
        jnp.broadcast_to(col_ang[None, :, :], (rows, GRID_W, inv.shape[0])),
    ], axis=-1)
    return ang.reshape(rows * GRID_W, ROPE_AXIS_DIM)


def apply_rope(x, cos, sin):
    xf = x.astype(jnp.float32)
    x1, x2 = xf[..., :ROPE_AXIS_DIM], xf[..., ROPE_AXIS_DIM:]
    c, s = cos[None, :, None, :], sin[None, :, None, :]
    return jnp.concatenate([x1 * c - x2 * s, x2 * c + x1 * s], axis=-1).astype(x.dtype)


def block_attention(q, k, v):
    b, s_len = q.shape[0], q.shape[1]
    n_blk = s_len // Q_BLOCK
    scale = HEAD_DIM ** -0.5
    qb = q.reshape(b, n_blk, Q_BLOCK, N_KV_HEADS, KV_GROUP, HEAD_DIM).transpose(1, 0, 2, 3, 4, 5)

    def one_block(q_blk):
        s = jnp.einsum('bqkgd,bskd->bkgqs', q_blk, k).astype(jnp.float32) * scale
        p = jax.nn.softmax(s, axis=-1).astype(v.dtype)
        return jnp.einsum('bkgqs,bskd->bqkgd', p, v)

    o = lax.map(one_block, qb)
    return o.transpose(1, 0, 2, 3, 4, 5).reshape(b, s_len, ATTN_WIDTH)


def short_conv(u, w):
    s_len = u.shape[1]
    up = jnp.pad(u, ((0, 0), (1, 1), (0, 0)))
    return w[0] * up[:, 0:s_len] + w[1] * up[:, 1:s_len + 1] + w[2] * up[:, 2:s_len + 2]


def encoder_layer(x, cos, sin, ffn1_pre, ffn1_post, w1_gate, w1_up, w1_down,
                  mix_pre, mix_post, w_in, q_norm, k_norm, conv_w, attn_out_norm,
                  conv_out_norm, w_out, ffn2_pre, ffn2_post, w2_gate, w2_up, w2_down):
    b, s_len = x.shape[0], x.shape[1]
    h = swiglu(rms_norm(x, ffn1_pre), w1_gate, w1_up, w1_down)
    x = x + FFN_RES_SCALE * rms_norm(h, ffn1_post)
    h = rms_norm(x, mix_pre)
    proj = h @ w_in
    o1 = ATTN_WIDTH
    o2 = o1 + KV_WIDTH
    o3 = o2 + KV_WIDTH
    o4 = o3 + CONV_WIDTH
    o5 = o4 + CONV_WIDTH
    q = proj[..., :o1].reshape(b, s_len, N_Q_HEADS, HEAD_DIM)
    k = proj[..., o1:o2].reshape(b, s_len, N_KV_HEADS, HEAD_DIM)
    v = proj[..., o2:o3].reshape(b, s_len, N_KV_HEADS, HEAD_DIM)
    gate_b = proj[..., o3:o4]
    gate_c = proj[..., o4:o5]
    h_conv = proj[..., o5:]
    q = apply_rope(rms_norm(q, q_norm), cos, sin)
    k = apply_rope(rms_norm(k, k_norm), cos, sin)
    attn = rms_norm(block_attention(q, k, v), attn_out_norm)
    conv = rms_norm(gate_b * short_conv(gate_c * h_conv, conv_w), conv_out_norm)
    mixed = jnp.concatenate([attn, conv], axis=-1) @ w_out
    x = x + rms_norm(mixed, mix_post)
    h = swiglu(rms_norm(x, ffn2_pre), w2_gate, w2_up, w2_down)
    x = x + FFN_RES_SCALE * rms_norm(h, ffn2_post)
    return x


def run_trunk(x, ffn1_pre, ffn1_post, w1_gate, w1_up, w1_down, mix_pre, mix_post,
              w_in, q_norm, k_norm, conv_w, attn_out_norm, conv_out_norm, w_out,
              ffn2_pre, ffn2_post, w2_gate, w2_up, w2_down):
    ang = axial_rope_angles(x.shape[1])
    cos, sin = jnp.cos(ang), jnp.sin(ang)
    for l in range(DEPTH):
        x = encoder_layer(x, cos, sin, ffn1_pre[l], ffn1_post[l], w1_gate[l], w1_up[l], w1_down[l],
                          mix_pre[l], mix_post[l], w_in[l], q_norm[l], k_norm[l], conv_w[l],
                          attn_out_norm[l], conv_out_norm[l], w_out[l],
                          ffn2_pre[l], ffn2_post[l], w2_gate[l], w2_up[l], w2_down[l])
    return x


def setup_inputs(seed: int = 0) -> dict:
    key = jax.random.key(seed)
    ks = jax.random.split(key, 24)

    def w(k, shape, fan_in):
        return jax.random.normal(k, shape, jnp.float32) * (fan_in ** -0.5)

    def gain(k, n):
        return 1.0 + 0.1 * jax.random.normal(k, (DEPTH, n), jnp.float32)

    return {
        'x_prompt': jax.random.normal(ks[0], (BATCH, SEQ, D_MODEL), jnp.float32),
        'x_sample': jax.random.normal(ks[1], (DEC_BATCH, DEC_SEQ, D_MODEL), jnp.float32),
        'ffn1_pre': gain(ks[2], D_MODEL),
        'ffn1_post': gain(ks[3], D_MODEL),
        'w1_gate': w(ks[4], (DEPTH, D_MODEL, D_FF), D_MODEL),
        'w1_up': w(ks[5], (DEPTH, D_MODEL, D_FF), D_MODEL),
        'w1_down': w(ks[6], (DEPTH, D_FF, D_MODEL), D_FF),
        'mix_pre': gain(ks[7], D_MODEL),
        'mix_post': gain(ks[8], D_MODEL),
        'w_in': w(ks[9], (DEPTH, D_MODEL, IN_PROJ_WIDTH), D_MODEL),
        'q_norm': gain(ks[10], HEAD_DIM),
        'k_norm': gain(ks[11], HEAD_DIM),
        'conv_w': w(ks[12], (DEPTH, CONV_K, CONV_WIDTH), CONV_K),
        'attn_out_norm': gain(ks[13], ATTN_WIDTH),
        'conv_out_norm': gain(ks[14], CONV_WIDTH),
        'w_out': w(ks[15], (DEPTH, D_MIX, D_MODEL), D_MIX),
        'ffn2_pre': gain(ks[16], D_MODEL),
        'ffn2_post': gain(ks[17], D_MODEL),
        'w2_gate': w(ks[18], (DEPTH, D_MODEL, D_FF), D_MODEL),
        'w2_up': w(ks[19], (DEPTH, D_MODEL, D_FF), D_MODEL),
        'w2_down': w(ks[20], (DEPTH, D_FF, D_MODEL), D_FF),
    }


def reference(x_prompt, x_sample, ffn1_pre, ffn1_post, w1_gate, w1_up, w1_down,
              mix_pre, mix_post, w_in, q_norm, k_norm, conv_w, attn_out_norm,
              conv_out_norm, w_out, ffn2_pre, ffn2_post, w2_gate, w2_up, w2_down):
    y_prompt = run_trunk(x_prompt, ffn1_pre, ffn1_post, w1_gate, w1_up, w1_down, mix_pre, mix_post,
                         w_in, q_norm, k_norm, conv_w, attn_out_norm, conv_out_norm, w_out,
                         ffn2_pre, ffn2_post, w2_gate, w2_up, w2_down)
    y_sample = run_trunk(x_sample, ffn1_pre, ffn1_post, w1_gate, w1_up, w1_down, mix_pre, mix_post,
                         w_in, q_norm, k_norm, conv_w, attn_out_norm, conv_out_norm, w_out,
                         ffn2_pre, ffn2_post, w2_gate, w2_up, w2_down)
    return (y_prompt, y_sample)
```

```python
import functools

import jax
import jax.numpy as jnp
from jax import lax
from jax.experimental import pallas as pl
from jax.experimental.pallas import tpu as pltpu

F32 = jnp.float32
BF16 = jnp.bfloat16

NORM_EPS = 1e-6
FFN_RES_SCALE = 0.5
HEAD_DIM = 128
N_Q_HEADS = 16
N_KV_HEADS = 4
KV_GROUP = N_Q_HEADS // N_KV_HEADS
ATTN_WIDTH = N_Q_HEADS * HEAD_DIM
KV_WIDTH = N_KV_HEADS * HEAD_DIM
CONV_WIDTH = 2048
GRID_W = 64
ROPE_THETA = 10000.0
ROPE_AXIS_DIM = HEAD_DIM // 2

LANE = 128
BF16_SUBLANES = 16
D_FF_PAD_MULTIPLE = 1024
VMEM_LIMIT_BYTES = 56 * 1024 * 1024


def _cparams(sem):
    return pltpu.CompilerParams(dimension_semantics=sem, vmem_limit_bytes=VMEM_LIMIT_BYTES)


def _rms(x):
    return x * lax.rsqrt(jnp.mean(x * x, axis=-1, keepdims=True) + NORM_EPS)


def _norm_kernel(x_ref, g_ref, o_ref):
    o_ref[...] = (_rms(x_ref[...]) * g_ref[...]).astype(o_ref.dtype)


def rms_norm_rows(x, g, tm=256):
    t, d = x.shape
    tm = min(tm, t)
    return pl.pallas_call(
        _norm_kernel,
        grid=(t // tm,),
        in_specs=[pl.BlockSpec((tm, d), lambda i: (i, 0)),
                  pl.BlockSpec((1, d), lambda i: (0, 0))],
        out_specs=pl.BlockSpec((tm, d), lambda i: (i, 0)),
        out_shape=jax.ShapeDtypeStruct((t, d), BF16),
        compiler_params=_cparams(("parallel",)),
        name="rms_norm_rows",
    )(x, g.reshape(1, d))


def _resid_norm_kernel(y_ref, x_ref, gp_ref, gn_ref, xo_ref, ho_ref, *, scale):
    x_new = x_ref[...] + scale * (_rms(y_ref[...]) * gp_ref[...])
    xo_ref[...] = x_new
    ho_ref[...] = (_rms(x_new) * gn_ref[...]).astype(ho_ref.dtype)


def _resid_kernel(y_ref, x_ref, gp_ref, xo_ref, *, scale):
    xo_ref[...] = x_ref[...] + scale * (_rms(y_ref[...]) * gp_ref[...])


def resid_norm_rows(y, x, g_post, scale, g_next=None, tm=256):
    t, d = x.shape
    tm = min(tm, t)
    row = pl.BlockSpec((tm, d), lambda i: (i, 0))
    vec = pl.BlockSpec((1, d), lambda i: (0, 0))
    if g_next is None:
        return pl.pallas_call(
            functools.partial(_resid_kernel, scale=scale),
            grid=(t // tm,),
            in_specs=[row, row, vec],
            out_specs=row,
            out_shape=jax.ShapeDtypeStruct((t, d), F32),
            compiler_params=_cparams(("parallel",)),
            name="resid_rows",
        )(y, x, g_post.reshape(1, d)), None
    return pl.pallas_call(
        functools.partial(_resid_norm_kernel, scale=scale),
        grid=(t // tm,),
        in_specs=[row, row, vec, vec],
        out_specs=[row, row],
        out_shape=[jax.ShapeDtypeStruct((t, d), F32), jax.ShapeDtypeStruct((t, d), BF16)],
        compiler_params=_cparams(("parallel",)),
        name="resid_norm_rows",
    )(y, x, g_post.reshape(1, d), g_next.reshape(1, d))


def _gateup_kernel(a_ref, wg_ref, wu_ref, o_ref):
    a = a_ref[...]
    g = jnp.dot(a, wg_ref[...], preferred_element_type=F32)
    u = jnp.dot(a, wu_ref[...], preferred_element_type=F32)
    o_ref[...] = (g * jax.nn.sigmoid(g) * u).astype(o_ref.dtype)


def gate_up(a, wg, wu, tm=1024, tn=512):
    t, d = a.shape
    f = wg.shape[1]
    tm, tn = min(tm, t), min(tn, f)
    return pl.pallas_call(
        _gateup_kernel,
        grid=(t // tm, f // tn),
        in_specs=[pl.BlockSpec((tm, d), lambda i, j: (i, 0)),
                  pl.BlockSpec((d, tn), lambda i, j: (0, j)),
                  pl.BlockSpec((d, tn), lambda i, j: (0, j))],
        out_specs=pl.BlockSpec((tm, tn), lambda i, j: (i, j)),
        out_shape=jax.ShapeDtypeStruct((t, f), BF16),
        compiler_params=_cparams(("parallel", "arbitrary")),
        name="gate_up",
    )(a, wg, wu)


def _mm_kernel(a_ref, w_ref, o_ref, acc_ref):
    k = pl.program_id(2)
    part = jnp.dot(a_ref[...], w_ref[...], preferred_element_type=F32)

    @pl.when(k == 0)
    def _():
        acc_ref[...] = part

    @pl.when(k > 0)
    def _():
        acc_ref[...] += part

    @pl.when(k == pl.num_programs(2) - 1)
    def _():
        o_ref[...] = acc_ref[...]


def _mm_fullk_kernel(a_ref, w_ref, o_ref):
    o_ref[...] = jnp.dot(a_ref[...], w_ref[...], preferred_element_type=F32)


def matmul(a, w, tm=1024, tn=1024, tk=None):
    t, kd = a.shape
    n = w.shape[1]
    tm, tn = min(tm, t), min(tn, n)
    if tk is None or tk >= kd:
        return pl.pallas_call(
            _mm_fullk_kernel,
            grid=(t // tm, n // tn),
            in_specs=[pl.BlockSpec((tm, kd), lambda i, j: (i, 0)),
                      pl.BlockSpec((kd, tn), lambda i, j: (0, j))],
            out_specs=pl.BlockSpec((tm, tn), lambda i, j: (i, j)),
            out_shape=jax.ShapeDtypeStruct((t, n), F32),
            compiler_params=_cparams(("parallel", "arbitrary")),
            name="matmul_fullk",
        )(a, w)
    return pl.pallas_call(
        _mm_kernel,
        grid=(t // tm, n // tn, kd // tk),
        in_specs=[pl.BlockSpec((tm, tk), lambda i, j, k: (i, k)),
                  pl.BlockSpec((tk, tn), lambda i, j, k: (k, j))],
        out_specs=pl.BlockSpec((tm, tn), lambda i, j, k: (i, j)),
        out_shape=jax.ShapeDtypeStruct((t, n), F32),
        scratch_shapes=[pltpu.VMEM((tm, tn), F32)],
        compiler_params=_cparams(("parallel", "arbitrary", "arbitrary")),
        name="matmul_kgrid",
    )(a, w)


QKV_TN = KV_GROUP * HEAD_DIM
N_QK_TILES = (ATTN_WIDTH + KV_WIDTH) // QKV_TN


def _inproj_qkv_kernel(a_ref, w_ref, qg_ref, kg_ref, cos_ref, sin_ref, o_ref):
    j = pl.program_id(1)
    acc = jnp.dot(a_ref[...], w_ref[...], preferred_element_type=F32)

    @pl.when(j < N_QK_TILES)
    def _():
        gain = jnp.where(j < ATTN_WIDTH // QKV_TN, qg_ref[...], kg_ref[...])
        cos, sin = cos_ref[...], sin_ref[...]
        for h in range(KV_GROUP):
            y = _rms(acc[:, h * HEAD_DIM:(h + 1) * HEAD_DIM]) * gain
            y = y * cos + pltpu.roll(y, ROPE_AXIS_DIM, axis=1) * sin
            o_ref[h] = y.astype(o_ref.dtype)

    @pl.when(j >= N_QK_TILES)
    def _():
        for h in range(KV_GROUP):
            o_ref[h] = acc[:, h * HEAD_DIM:(h + 1) * HEAD_DIM].astype(o_ref.dtype)


def inproj_qkv(a, w_in, q_norm, k_norm, cos_full, sin_signed, seq_len, tm=1024):
    t, d = a.shape
    tm = min(tm, seq_len)
    n_seq_tiles = seq_len // tm
    n_tiles = N_QK_TILES + KV_WIDTH // QKV_TN
    return pl.pallas_call(
        _inproj_qkv_kernel,
        grid=(t // tm, n_tiles),
        in_specs=[pl.BlockSpec((tm, d), lambda i, j: (i, 0)),
                  pl.BlockSpec((d, QKV_TN), lambda i, j: (0, j)),
                  pl.BlockSpec((1, HEAD_DIM), lambda i, j: (0, 0)),
                  pl.BlockSpec((1, HEAD_DIM), lambda i, j: (0, 0)),
                  pl.BlockSpec((tm, HEAD_DIM), lambda i, j: (i % n_seq_tiles, 0)),
                  pl.BlockSpec((tm, HEAD_DIM), lambda i, j: (i % n_seq_tiles, 0))],
        out_specs=pl.BlockSpec((KV_GROUP, tm, HEAD_DIM), lambda i, j: (j, i, 0)),
        out_shape=jax.ShapeDtypeStruct((n_tiles * KV_GROUP, t, HEAD_DIM), BF16),
        compiler_params=_cparams(("parallel", "arbitrary")),
        name="inproj_qkv",
    )(a, w_in, q_norm.reshape(1, HEAD_DIM), k_norm.reshape(1, HEAD_DIM), cos_full, sin_signed)


def _inproj_conv_kernel(a_ref, wb_ref, wc_ref, wh_ref, gb_ref, u_ref):
    a = a_ref[...]
    gb_ref[...] = jnp.dot(a, wb_ref[...], preferred_element_type=F32).astype(gb_ref.dtype)
    c = jnp.dot(a, wc_ref[...], preferred_element_type=F32)
    h = jnp.dot(a, wh_ref[...], preferred_element_type=F32)
    u_ref[...] = (c * h).astype(u_ref.dtype)


def inproj_conv(a, w_in, tm=1024, tn=512):
    t, d = a.shape
    tm = min(tm, t)
    off_b = (ATTN_WIDTH + 2 * KV_WIDTH) // tn
    off_c = off_b + CONV_WIDTH // tn
    off_h = off_c + CONV_WIDTH // tn
    out = jax.ShapeDtypeStruct((t, CONV_WIDTH), BF16)
    return pl.pallas_call(
        _inproj_conv_kernel,
        grid=(t // tm, CONV_WIDTH // tn),
        in_specs=[pl.BlockSpec((tm, d), lambda i, j: (i, 0)),
                  pl.BlockSpec((d, tn), lambda i, j: (0, off_b + j)),
                  pl.BlockSpec((d, tn), lambda i, j: (0, off_c + j)),
                  pl.BlockSpec((d, tn), lambda i, j: (0, off_h + j))],
        out_specs=[pl.BlockSpec((tm, tn), lambda i, j: (i, j)),
                   pl.BlockSpec((tm, tn), lambda i, j: (i, j))],
        out_shape=[out, out],
        compiler_params=_cparams(("parallel", "arbitrary")),
        name="inproj_conv",
    )(a, w_in, w_in, w_in)


def _attn_kernel(q_ref, k_ref, v_ref, o_ref, *, tk, scale):
    g, tq, dh = q_ref.shape
    q = q_ref[...].reshape(g * tq, dh)
    n_chunks = k_ref.shape[1] // tk

    def body(c, carry):
        m, l, acc = carry
        start = pl.multiple_of(c * tk, tk)
        kc = k_ref[0, pl.ds(start, tk), :]
        vc = v_ref[0, pl.ds(start, tk), :]
        s = lax.dot_general(q, kc, (((1,), (1,)), ((), ())), preferred_element_type=F32) * scale
        m_new = jnp.maximum(m, jnp.max(s, axis=-1, keepdims=True))
        alpha = jnp.exp(m - m_new)
        p = jnp.exp(s - m_new)
        l = alpha * l + jnp.sum(p, axis=-1, keepdims=True)
        acc = alpha * acc + jnp.dot(p.astype(vc.dtype), vc, preferred_element_type=F32)
        return m_new, l, acc

    m0 = jnp.full((g * tq, 1), -jnp.inf, F32)
    l0 = jnp.zeros((g * tq, 1), F32)
    acc0 = jnp.zeros((g * tq, dh), F32)
    _, l, acc = lax.fori_loop(0, n_chunks, body, (m0, l0, acc0))
    out = acc / l
    for h in range(g):
        o_ref[:, h * dh:(h + 1) * dh] = out[h * tq:(h + 1) * tq].astype(o_ref.dtype)


def attention(qkv, batch, seq_len, tq=256, tk=512):
    t = qkv.shape[1]
    tq, tk = min(tq, seq_len), min(tk, seq_len)
    nq = seq_len // tq
    kern = functools.partial(_attn_kernel, tk=tk, scale=HEAD_DIM ** -0.5)
    return pl.pallas_call(
        kern,
        grid=(batch, N_KV_HEADS, nq),
        in_specs=[pl.BlockSpec((KV_GROUP, tq, HEAD_DIM), lambda b, g, i: (g, b * nq + i, 0)),
                  pl.BlockSpec((1, seq_len, HEAD_DIM), lambda b, g, i: (N_Q_HEADS + g, b, 0)),
                  pl.BlockSpec((1, seq_len, HEAD_DIM),
                               lambda b, g, i: (N_Q_HEADS + N_KV_HEADS + g, b, 0))],
        out_specs=pl.BlockSpec((tq, KV_GROUP * HEAD_DIM), lambda b, g, i: (b * nq + i, g)),
        out_shape=jax.ShapeDtypeStruct((t, ATTN_WIDTH), BF16),
        compiler_params=_cparams(("parallel", "parallel", "arbitrary")),
        name="attention",
    )(qkv, qkv, qkv)


def _premix_kernel(attn_ref, gb_ref, u_ref, up_ref, un_ref, cw_ref, ga_ref, gc_ref, o_ref, *, n_seq_tiles):
    i = pl.program_id(0)
    tm = u_ref.shape[0]
    o_ref[:, :ATTN_WIDTH] = (_rms(attn_ref[...].astype(F32)) * ga_ref[...]).astype(o_ref.dtype)

    u = u_ref[...].astype(F32)
    first = (i % n_seq_tiles) == 0
    last = (i % n_seq_tiles) == n_seq_tiles - 1
    prev_row = jnp.where(first, 0.0, up_ref[BF16_SUBLANES - 1:BF16_SUBLANES, :].astype(F32))
    next_row = jnp.where(last, 0.0, un_ref[0:1, :].astype(F32))
    row = lax.broadcasted_iota(jnp.int32, u.shape, 0)
    u_prev = jnp.where(row == 0, prev_row, pltpu.roll(u, 1, axis=0))
    u_next = jnp.where(row == tm - 1, next_row, pltpu.roll(u, tm - 1, axis=0))
    cw = cw_ref[...]
    conv = gb_ref[...].astype(F32) * (cw[0:1] * u_prev + cw[1:2] * u + cw[2:3] * u_next)
    o_ref[:, ATTN_WIDTH:] = (_rms(conv) * gc_ref[...]).astype(o_ref.dtype)


def premix(attn, gate_b, u, conv_w, g_attn, g_conv, seq_len, tm=256):
    t = attn.shape[0]
    tm = min(tm, seq_len)
    n_seq_tiles = seq_len // tm
    halo = tm // BF16_SUBLANES
    n_halo = t // BF16_SUBLANES
    row = lambda w: pl.BlockSpec((tm, w), lambda i: (i, 0))
    vec = lambda w: pl.BlockSpec((1, w), lambda i: (0, 0))
    return pl.pallas_call(
        functools.partial(_premix_kernel, n_seq_tiles=n_seq_tiles),
        grid=(t // tm,),
        in_specs=[row(ATTN_WIDTH), row(CONV_WIDTH), row(CONV_WIDTH),
                  pl.BlockSpec((BF16_SUBLANES, CONV_WIDTH), lambda i: (jnp.maximum(i * halo - 1, 0), 0)),
                  pl.BlockSpec((BF16_SUBLANES, CONV_WIDTH),
                               lambda i: (jnp.minimum((i + 1) * halo, n_halo - 1), 0)),
                  pl.BlockSpec((3, CONV_WIDTH), lambda i: (0, 0)),
                  vec(ATTN_WIDTH), vec(CONV_WIDTH)],
        out_specs=row(ATTN_WIDTH + CONV_WIDTH),
        out_shape=jax.ShapeDtypeStruct((t, ATTN_WIDTH + CONV_WIDTH), BF16),
        compiler_params=_cparams(("parallel",)),
        name="premix",
    )(attn, gate_b, u, u, u, conv_w, g_attn.reshape(1, -1), g_conv.reshape(1, -1))


def _rope_tables(seq_len):
    rows = seq_len // GRID_W
    inv = 1.0 / (ROPE_THETA ** (jnp.arange(0, ROPE_AXIS_DIM, 2, dtype=F32) / ROPE_AXIS_DIM))
    row_ang = jnp.arange(rows, dtype=F32)[:, None] * inv
    col_ang = jnp.arange(GRID_W, dtype=F32)[:, None] * inv
    ang = jnp.concatenate([
        jnp.broadcast_to(row_ang[:, None, :], (rows, GRID_W, inv.shape[0])),
        jnp.broadcast_to(col_ang[None, :, :], (rows, GRID_W, inv.shape[0])),
    ], axis=-1).reshape(seq_len, ROPE_AXIS_DIM)
    cos, sin = jnp.cos(ang), jnp.sin(ang)
    return jnp.concatenate([cos, cos], axis=-1), jnp.concatenate([-sin, sin], axis=-1)


def _prep_ffn(w_gate, w_up, w_down):
    f = w_gate.shape[1]
    f_pad = -(-f // D_FF_PAD_MULTIPLE) * D_FF_PAD_MULTIPLE
    pad_cols = lambda w: jnp.pad(w.astype(BF16), ((0, 0), (0, f_pad - f)))
    return pad_cols(w_gate), pad_cols(w_up), jnp.pad(w_down.astype(BF16), ((0, f_pad - f), (0, 0)))


def _ffn(hn, x, w, g_post, g_next):
    wg, wu, wd = w
    h = gate_up(hn, wg, wu)
    y = matmul(h, wd, tk=wd.shape[0] // 4 if wd.shape[0] % (4 * LANE) == 0 else None)
    return resid_norm_rows(y, x, g_post, FFN_RES_SCALE, g_next)


def _trunk(x3d, p):
    b, s, d = x3d.shape
    x = x3d.reshape(b * s, d)
    cos_full, sin_signed = _rope_tables(s)
    hn = rms_norm_rows(x, p["ffn1_pre"])
    x, hn = _ffn(hn, x, p["ffn1"], p["ffn1_post"], p["mix_pre"])
    qkv = inproj_qkv(hn, p["w_in"], p["q_norm"], p["k_norm"], cos_full, sin_signed, s)
    gate_b, u = inproj_conv(hn, p["w_in"])
    attn = attention(qkv, b, s)
    mix_in = premix(attn, gate_b, u, p["conv_w"], p["attn_out_norm"], p["conv_out_norm"], s)
    y = matmul(mix_in, p["w_out"])
    x, hn = resid_norm_rows(y, x, p["mix_post"], 1.0, p["ffn2_pre"])
    x, _ = _ffn(hn, x, p["ffn2"], p["ffn2_post"], None)
    return x.reshape(b, s, d)


def kernel(x_prompt, x_sample, ffn1_pre, ffn1_post, w1_gate, w1_up, w1_down, mix_pre, mix_post, w_in, q_norm, k_norm, conv_w, attn_out_norm, conv_out_norm, w_out, ffn2_pre, ffn2_post, w2_gate, w2_up, w2_down):
    p = {
        "ffn1_pre": ffn1_pre[0], "ffn1_post": ffn1_post[0],
        "ffn1": _prep_ffn(w1_gate[0], w1_up[0], w1_down[0]),
        "mix_pre": mix_pre[0], "mix_post": mix_post[0],
        "w_in": w_in[0].astype(BF16), "q_norm": q_norm[0], "k_norm": k_norm[0],
        "conv_w": conv_w[0], "attn_out_norm": attn_out_norm[0], "conv_out_norm": conv_out_norm[0],
        "w_out": w_out[0].astype(BF16),
        "ffn2_pre": ffn2_pre[0], "ffn2_post": ffn2_post[0],
        "ffn2": _prep_ffn(w2_gate[0], w2_up[0], w2_down[0]),
    }
    return _trunk(x_prompt, p), _trunk(x_sample, p)
```

```python
import functools

import jax
import jax.numpy as jnp
from jax import lax
from jax.experimental import pallas as pl
from jax.experimental.pallas import tpu as pltpu

F32 = jnp.float32
BF16 = jnp.bfloat16

NORM_EPS = 1e-6
FFN_RES_SCALE = 0.5
HEAD_DIM = 128
N_Q_HEADS = 16
N_KV_HEADS = 4
KV_GROUP = N_Q_HEADS // N_KV_HEADS
ATTN_WIDTH = N_Q_HEADS * HEAD_DIM
KV_WIDTH = N_KV_HEADS * HEAD_DIM
CONV_WIDTH = 2048
GRID_W = 64
ROPE_THETA = 10000.0
ROPE_AXIS_DIM = HEAD_DIM // 2

LANE = 128
BF16_SUBLANES = 16
FFN_TN = 256
VMEM_LIMIT_BYTES = 56 * 1024 * 1024


def _cparams(sem):
    return pltpu.CompilerParams(dimension_semantics=sem, vmem_limit_bytes=VMEM_LIMIT_BYTES)


def _rms(x):
    return x * lax.rsqrt(jnp.mean(x * x, axis=-1, keepdims=True) + NORM_EPS)


def _norm_kernel(x_ref, g_ref, o_ref):
    o_ref[...] = (_rms(x_ref[...]) * g_ref[...]).astype(o_ref.dtype)


def rms_norm_rows(x, g, tm=256):
    t, d = x.shape
    tm = min(tm, t)
    return pl.pallas_call(
        _norm_kernel,
        grid=(t // tm,),
        in_specs=[pl.BlockSpec((tm, d), lambda i: (i, 0)),
                  pl.BlockSpec((1, d), lambda i: (0, 0))],
        out_specs=pl.BlockSpec((tm, d), lambda i: (i, 0)),
        out_shape=jax.ShapeDtypeStruct((t, d), BF16),
        compiler_params=_cparams(("parallel",)),
        name="rms_norm_rows",
    )(x, g.reshape(1, d))


def _resid_norm_kernel(y_ref, x_ref, gp_ref, gn_ref, xo_ref, ho_ref, *, scale):
    x_new = x_ref[...] + scale * (_rms(y_ref[...]) * gp_ref[...])
    xo_ref[...] = x_new
    ho_ref[...] = (_rms(x_new) * gn_ref[...]).astype(ho_ref.dtype)


def _resid_kernel(y_ref, x_ref, gp_ref, xo_ref, *, scale):
    xo_ref[...] = x_ref[...] + scale * (_rms(y_ref[...]) * gp_ref[...])


def resid_norm_rows(y, x, g_post, scale, g_next=None, tm=256):
    t, d = x.shape
    tm = min(tm, t)
    row = pl.BlockSpec((tm, d), lambda i: (i, 0))
    vec = pl.BlockSpec((1, d), lambda i: (0, 0))
    if g_next is None:
        return pl.pallas_call(
            functools.partial(_resid_kernel, scale=scale),
            grid=(t // tm,),
            in_specs=[row, row, vec],
            out_specs=row,
            out_shape=jax.ShapeDtypeStruct((t, d), F32),
            compiler_params=_cparams(("parallel",)),
            name="resid_rows",
        )(y, x, g_post.reshape(1, d)), None
    return pl.pallas_call(
        functools.partial(_resid_norm_kernel, scale=scale),
        grid=(t // tm,),
        in_specs=[row, row, vec, vec],
        out_specs=[row, row],
        out_shape=[jax.ShapeDtypeStruct((t, d), F32), jax.ShapeDtypeStruct((t, d), BF16)],
        compiler_params=_cparams(("parallel",)),
        name="resid_norm_rows",
    )(y, x, g_post.reshape(1, d), g_next.reshape(1, d))


def _gateup_kernel(a_ref, wg_ref, wu_ref, o_ref):
    a = a_ref[...]
    g = jnp.dot(a, wg_ref[...], preferred_element_type=F32)
    u = jnp.dot(a, wu_ref[...], preferred_element_type=F32)
    o_ref[...] = (g * jax.nn.sigmoid(g) * u).astype(o_ref.dtype)


def gate_up(a, wg, wu, tm=1024, tn=512):
    t, d = a.shape
    f = wg.shape[1]
    tm, tn = min(tm, t), min(tn, f)
    return pl.pallas_call(
        _gateup_kernel,
        grid=(t // tm, f // tn),
        in_specs=[pl.BlockSpec((tm, d), lambda i, j: (i, 0)),
                  pl.BlockSpec((d, tn), lambda i, j: (0, j)),
                  pl.BlockSpec((d, tn), lambda i, j: (0, j))],
        out_specs=pl.BlockSpec((tm, tn), lambda i, j: (i, j)),
        out_shape=jax.ShapeDtypeStruct((t, f), BF16),
        compiler_params=_cparams(("parallel", "arbitrary")),
        name="gate_up",
    )(a, wg, wu)


def _mm_kernel(a_ref, w_ref, o_ref, acc_ref):
    k = pl.program_id(2)
    part = jnp.dot(a_ref[...], w_ref[...], preferred_element_type=F32)

    @pl.when(k == 0)
    def _():
        acc_ref[...] = part

    @pl.when(k > 0)
    def _():
        acc_ref[...] += part

    @pl.when(k == pl.num_programs(2) - 1)
    def _():
        o_ref[...] = acc_ref[...]


def _mm_fullk_kernel(a_ref, w_ref, o_ref):
    o_ref[...] = jnp.dot(a_ref[...], w_ref[...], preferred_element_type=F32)


def matmul(a, w, tm=1024, tn=1024, tk=None):
    t, kd = a.shape
    n = w.shape[1]
    tm, tn = min(tm, t), min(tn, n)
    if tk is None or tk >= kd:
        return pl.pallas_call(
            _mm_fullk_kernel,
            grid=(t // tm, n // tn),
            in_specs=[pl.BlockSpec((tm, kd), lambda i, j: (i, 0)),
                      pl.BlockSpec((kd, tn), lambda i, j: (0, j))],
            out_specs=pl.BlockSpec((tm, tn), lambda i, j: (i, j)),
            out_shape=jax.ShapeDtypeStruct((t, n), F32),
            compiler_params=_cparams(("parallel", "arbitrary")),
            name="matmul_fullk",
        )(a, w)
    return pl.pallas_call(
        _mm_kernel,
        grid=(t // tm, n // tn, kd // tk),
        in_specs=[pl.BlockSpec((tm, tk), lambda i, j, k: (i, k)),
                  pl.BlockSpec((tk, tn), lambda i, j, k: (k, j))],
        out_specs=pl.BlockSpec((tm, tn), lambda i, j, k: (i, j)),
        out_shape=jax.ShapeDtypeStruct((t, n), F32),
        scratch_shapes=[pltpu.VMEM((tm, tn), F32)],
        compiler_params=_cparams(("parallel", "arbitrary", "arbitrary")),
        name="matmul_kgrid",
    )(a, w)


QKV_TN = KV_GROUP * HEAD_DIM
N_Q_TILES = ATTN_WIDTH // QKV_TN
ATTN_TQ = 256
ATTN_TK = 1024
Q_SCALE = (HEAD_DIM ** -0.5) * 1.4426950408889634


def _inproj_qkv_kernel(a_ref, w_ref, qg_ref, kg_ref, cos_ref, sin_ref, qt_ref, k_ref, vt_ref):
    j = pl.program_id(1)
    acc = jnp.dot(a_ref[...], w_ref[...], preferred_element_type=F32)
    tm = acc.shape[0]

    def head(h):
        return acc[:, h * HEAD_DIM:(h + 1) * HEAD_DIM]

    def norm_rope(y, gain):
        y = _rms(y) * gain
        return y * cos_ref[...] + pltpu.roll(y, ROPE_AXIS_DIM, axis=1) * sin_ref[...]

    def store_transposed(dst_ref, h, y, width):
        yt = y.T.astype(dst_ref.dtype)
        for c in range(tm // width):
            dst_ref[h, c] = yt[:, c * width:(c + 1) * width]

    @pl.when(j < N_Q_TILES)
    def _():
        for h in range(KV_GROUP):
            store_transposed(qt_ref, h, norm_rope(head(h), qg_ref[...]) * Q_SCALE, ATTN_TQ)

    @pl.when(j == N_Q_TILES)
    def _():
        for h in range(KV_GROUP):
            k_ref[h] = norm_rope(head(h), kg_ref[...]).astype(k_ref.dtype)

    @pl.when(j == N_Q_TILES + 1)
    def _():
        for h in range(KV_GROUP):
            store_transposed(vt_ref, h, head(h), ATTN_TK)


def inproj_qkv(a, w_in, q_norm, k_norm, cos_full, sin_signed, seq_len, tm=1024):
    t, d = a.shape
    tm = min(tm, seq_len)
    n_seq_tiles = seq_len // tm
    return pl.pallas_call(
        _inproj_qkv_kernel,
        grid=(t // tm, N_Q_TILES + 2),
        in_specs=[pl.BlockSpec((tm, d), lambda i, j: (i, 0)),
                  pl.BlockSpec((d, QKV_TN), lambda i, j: (0, j)),
                  pl.BlockSpec((1, HEAD_DIM), lambda i, j: (0, 0)),
                  pl.BlockSpec((1, HEAD_DIM), lambda i, j: (0, 0)),
                  pl.BlockSpec((tm, HEAD_DIM), lambda i, j: (i % n_seq_tiles, 0)),
                  pl.BlockSpec((tm, HEAD_DIM), lambda i, j: (i % n_seq_tiles, 0))],
        out_specs=[pl.BlockSpec((KV_GROUP, tm // ATTN_TQ, HEAD_DIM, ATTN_TQ),
                                lambda i, j: (jnp.minimum(j, N_Q_TILES - 1), i, 0, 0)),
                   pl.BlockSpec((N_KV_HEADS, tm, HEAD_DIM), lambda i, j: (0, i, 0)),
                   pl.BlockSpec((N_KV_HEADS, tm // ATTN_TK, HEAD_DIM, ATTN_TK), lambda i, j: (0, i, 0, 0))],
        out_shape=[jax.ShapeDtypeStruct((N_Q_HEADS, t // ATTN_TQ, HEAD_DIM, ATTN_TQ), BF16),
                   jax.ShapeDtypeStruct((N_KV_HEADS, t, HEAD_DIM), BF16),
                   jax.ShapeDtypeStruct((N_KV_HEADS, t // ATTN_TK, HEAD_DIM, ATTN_TK), BF16)],
        compiler_params=_cparams(("parallel", "arbitrary")),
        name="inproj_qkv",
    )(a, w_in, q_norm.reshape(1, HEAD_DIM), k_norm.reshape(1, HEAD_DIM), cos_full, sin_signed)


def _inproj_conv_kernel(a_ref, wb_ref, wc_ref, wh_ref, gb_ref, u_ref):
    a = a_ref[...]
    gb_ref[...] = jnp.dot(a, wb_ref[...], preferred_element_type=F32).astype(gb_ref.dtype)
    c = jnp.dot(a, wc_ref[...], preferred_element_type=F32)
    h = jnp.dot(a, wh_ref[...], preferred_element_type=F32)
    u_ref[...] = (c * h).astype(u_ref.dtype)


def inproj_conv(a, w_in, tm=1024, tn=512):
    t, d = a.shape
    tm = min(tm, t)
    off_b = (ATTN_WIDTH + 2 * KV_WIDTH) // tn
    off_c = off_b + CONV_WIDTH // tn
    off_h = off_c + CONV_WIDTH // tn
    out = jax.ShapeDtypeStruct((t, CONV_WIDTH), BF16)
    return pl.pallas_call(
        _inproj_conv_kernel,
        grid=(t // tm, CONV_WIDTH // tn),
        in_specs=[pl.BlockSpec((tm, d), lambda i, j: (i, 0)),
                  pl.BlockSpec((d, tn), lambda i, j: (0, off_b + j)),
                  pl.BlockSpec((d, tn), lambda i, j: (0, off_c + j)),
                  pl.BlockSpec((d, tn), lambda i, j: (0, off_h + j))],
        out_specs=[pl.BlockSpec((tm, tn), lambda i, j: (i, j)),
                   pl.BlockSpec((tm, tn), lambda i, j: (i, j))],
        out_shape=[out, out],
        compiler_params=_cparams(("parallel", "arbitrary")),
        name="inproj_conv",
    )(a, w_in, w_in, w_in)


def _attn_kernel(qt_ref, k_ref, vt_ref, o_ref, m_ref, l_ref, acc_ref, s_ref, p_ref, alpha_ref):
    n_heads = qt_ref.shape[0]
    n_chunks = vt_ref.shape[1]
    tk = vt_ref.shape[3]

    def scores(c, h, slot):
        kc = k_ref[0, pl.ds(pl.multiple_of(c * tk, tk), tk), :]
        s_ref[slot] = jnp.dot(kc, qt_ref[h, 0], preferred_element_type=F32)

    def softmax(h, slot):
        m_old = m_ref[h]
        m_new = jnp.maximum(m_old, jnp.max(s_ref[slot], axis=0, keepdims=True))
        alpha = jnp.exp2(m_old - m_new)
        p = jnp.exp2(s_ref[slot] - m_new)
        l_ref[h] = alpha * l_ref[h] + jnp.sum(p, axis=0, keepdims=True)
        m_ref[h] = m_new
        alpha_ref[slot] = alpha
        p_ref[slot] = p.astype(p_ref.dtype)

    def values(c, h, slot):
        pv = jnp.dot(vt_ref[0, c], p_ref[slot], preferred_element_type=F32)
        acc_ref[h] = alpha_ref[slot] * acc_ref[h] + pv

    m_ref[...] = jnp.full(m_ref.shape, -jnp.inf, F32)
    l_ref[...] = jnp.zeros(l_ref.shape, F32)
    acc_ref[...] = jnp.zeros(acc_ref.shape, F32)
    p_ref[...] = jnp.zeros(p_ref.shape, p_ref.dtype)
    alpha_ref[...] = jnp.ones(alpha_ref.shape, F32)
    scores(0, 0, 0)

    @pl.loop(0, n_chunks)
    def _(c):
        for h in range(n_heads):
            slot, other = h % 2, (h + 1) % 2
            if h + 1 < n_heads:
                scores(c, h + 1, other)
            else:
                scores(jnp.minimum(c + 1, n_chunks - 1), 0, other)
            if h > 0:
                values(c, h - 1, other)
            else:
                values(jnp.maximum(c - 1, 0), n_heads - 1, other)
            softmax(h, slot)

    values(n_chunks - 1, n_heads - 1, (n_heads - 1) % 2)
    for h in range(n_heads):
        out_t = acc_ref[h] / l_ref[h]
        o_ref[:, h * HEAD_DIM:(h + 1) * HEAD_DIM] = out_t.T.astype(o_ref.dtype)


def attention(qt, k, vt, batch, seq_len):
    t = k.shape[1]
    nq = seq_len // ATTN_TQ
    nk = seq_len // ATTN_TK
    return pl.pallas_call(
        _attn_kernel,
        grid=(batch, N_KV_HEADS, nq),
        in_specs=[pl.BlockSpec((KV_GROUP, 1, HEAD_DIM, ATTN_TQ), lambda b, g, i: (g, b * nq + i, 0, 0)),
                  pl.BlockSpec((1, seq_len, HEAD_DIM), lambda b, g, i: (g, b, 0)),
                  pl.BlockSpec((1, nk, HEAD_DIM, ATTN_TK), lambda b, g, i: (g, b, 0, 0))],
        out_specs=pl.BlockSpec((ATTN_TQ, KV_GROUP * HEAD_DIM), lambda b, g, i: (b * nq + i, g)),
        out_shape=jax.ShapeDtypeStruct((t, ATTN_WIDTH), BF16),
        scratch_shapes=[pltpu.VMEM((KV_GROUP, 1, ATTN_TQ), F32),
                        pltpu.VMEM((KV_GROUP, 1, ATTN_TQ), F32),
                        pltpu.VMEM((KV_GROUP, HEAD_DIM, ATTN_TQ), F32),
                        pltpu.VMEM((2, ATTN_TK, ATTN_TQ), F32),
                        pltpu.VMEM((2, ATTN_TK, ATTN_TQ), BF16),
                        pltpu.VMEM((2, 1, ATTN_TQ), F32)],
        compiler_params=_cparams(("parallel", "parallel", "arbitrary")),
        name="attention",
    )(qt, k, vt)


def _premix_kernel(attn_ref, gb_ref, u_ref, up_ref, un_ref, cw_ref, ga_ref, gc_ref, o_ref, *, n_seq_tiles):
    i = pl.program_id(0)
    tm = u_ref.shape[0]
    o_ref[:, :ATTN_WIDTH] = (_rms(attn_ref[...].astype(F32)) * ga_ref[...]).astype(o_ref.dtype)

    u = u_ref[...].astype(F32)
    first = (i % n_seq_tiles) == 0
    last = (i % n_seq_tiles) == n_seq_tiles - 1
    prev_row = jnp.where(first, 0.0, up_ref[BF16_SUBLANES - 1:BF16_SUBLANES, :].astype(F32))
    next_row = jnp.where(last, 0.0, un_ref[0:1, :].astype(F32))
    row = lax.broadcasted_iota(jnp.int32, u.shape, 0)
    u_prev = jnp.where(row == 0, prev_row, pltpu.roll(u, 1, axis=0))
    u_next = jnp.where(row == tm - 1, next_row, pltpu.roll(u, tm - 1, axis=0))
    cw = cw_ref[...]
    conv = gb_ref[...].astype(F32) * (cw[0:1] * u_prev + cw[1:2] * u + cw[2:3] * u_next)
    o_ref[:, ATTN_WIDTH:] = (_rms(conv) * gc_ref[...]).astype(o_ref.dtype)


def premix(attn, gate_b, u, conv_w, g_attn, g_conv, seq_len, tm=256):
    t = attn.shape[0]
    tm = min(tm, seq_len)
    n_seq_tiles = seq_len // tm
    halo = tm // BF16_SUBLANES
    n_halo = t // BF16_SUBLANES
    row = lambda w: pl.BlockSpec((tm, w), lambda i: (i, 0))
    vec = lambda w: pl.BlockSpec((1, w), lambda i: (0, 0))
    return pl.pallas_call(
        functools.partial(_premix_kernel, n_seq_tiles=n_seq_tiles),
        grid=(t // tm,),
        in_specs=[row(ATTN_WIDTH), row(CONV_WIDTH), row(CONV_WIDTH),
                  pl.BlockSpec((BF16_SUBLANES, CONV_WIDTH), lambda i: (jnp.maximum(i * halo - 1, 0), 0)),
                  pl.BlockSpec((BF16_SUBLANES, CONV_WIDTH),
                               lambda i: (jnp.minimum((i + 1) * halo, n_halo - 1), 0)),
                  pl.BlockSpec((3, CONV_WIDTH), lambda i: (0, 0)),
                  vec(ATTN_WIDTH), vec(CONV_WIDTH)],
        out_specs=row(ATTN_WIDTH + CONV_WIDTH),
        out_shape=jax.ShapeDtypeStruct((t, ATTN_WIDTH + CONV_WIDTH), BF16),
        compiler_params=_cparams(("parallel",)),
        name="premix",
    )(attn, gate_b, u, u, u, conv_w, g_attn.reshape(1, -1), g_conv.reshape(1, -1))


def _rope_tables(seq_len):
    rows = seq_len // GRID_W
    inv = 1.0 / (ROPE_THETA ** (jnp.arange(0, ROPE_AXIS_DIM, 2, dtype=F32) / ROPE_AXIS_DIM))
    row_ang = jnp.arange(rows, dtype=F32)[:, None] * inv
    col_ang = jnp.arange(GRID_W, dtype=F32)[:, None] * inv
    ang = jnp.concatenate([
        jnp.broadcast_to(row_ang[:, None, :], (rows, GRID_W, inv.shape[0])),
        jnp.broadcast_to(col_ang[None, :, :], (rows, GRID_W, inv.shape[0])),
    ], axis=-1).reshape(seq_len, ROPE_AXIS_DIM)
    cos, sin = jnp.cos(ang), jnp.sin(ang)
    return jnp.concatenate([cos, cos], axis=-1), jnp.concatenate([-sin, sin], axis=-1)


def _prep_ffn(w_gate, w_up, w_down):
    return w_gate.astype(BF16), w_up.astype(BF16), w_down.astype(BF16)


def _ffn(hn, x, w, g_post, g_next):
    wg, wu, wd = w
    h = gate_up(hn, wg, wu, tm=2048, tn=FFN_TN)
    y = matmul(h, wd, tm=512, tn=512)
    return resid_norm_rows(y, x, g_post, FFN_RES_SCALE, g_next)


def _trunk(x3d, p):
    b, s, d = x3d.shape
    x = x3d.reshape(b * s, d)
    cos_full, sin_signed = _rope_tables(s)
    hn = rms_norm_rows(x, p["ffn1_pre"])
    x, hn = _ffn(hn, x, p["ffn1"], p["ffn1_post"], p["mix_pre"])
    qt, k, vt = inproj_qkv(hn, p["w_in"], p["q_norm"], p["k_norm"], cos_full, sin_signed, s)
    gate_b, u = inproj_conv(hn, p["w_in"])
    attn = attention(qt, k, vt, b, s)
    mix_in = premix(attn, gate_b, u, p["conv_w"], p["attn_out_norm"], p["conv_out_norm"], s)
    y = matmul(mix_in, p["w_out"])
    x, hn = resid_norm_rows(y, x, p["mix_post"], 1.0, p["ffn2_pre"])
    x, _ = _ffn(hn, x, p["ffn2"], p["ffn2_post"], None)
    return x.reshape(b, s, d)


def kernel(x_prompt, x_sample, ffn1_pre, ffn1_post, w1_gate, w1_up, w1_down, mix_pre, mix_post, w_in, q_norm, k_norm, conv_w, attn_out_norm, conv_out_norm, w_out, ffn2_pre, ffn2_post, w2_gate, w2_up, w2_down):
    p = {
        "ffn1_pre": ffn1_pre[0], "ffn1_post": ffn1_post[0],
        "ffn1": _prep_ffn(w1_gate[0], w1_up[0], w1_down[0]),
        "mix_pre": mix_pre[0], "mix_post": mix_post[0],
        "w_in": w_in[0].astype(BF16), "q_norm": q_norm[0], "k_norm": k_norm[0],
        "conv_w": conv_w[0], "attn_out_norm": attn_out_norm[0], "conv_out_norm": conv_out_norm[0],
        "w_out": w_out[0].astype(BF16),
        "ffn2_pre": ffn2_pre[0], "ffn2_post": ffn2_post[0],
        "ffn2": _prep_ffn(w2_gate[0], w2_up[0], w2_down[0]),
    }
    return _trunk(x_prompt, p), _trunk(x_sample, p)
```

```python
import functools

import jax
import jax.numpy as jnp
from jax import lax
from jax.experimental import pallas as pl
from jax.experimental.pallas import tpu as pltpu

F32 = jnp.float32
BF16 = jnp.bfloat16

NORM_EPS = 1e-6
FFN_RES_SCALE = 0.5
HEAD_DIM = 128
N_Q_HEADS = 16
N_KV_HEADS = 4
KV_GROUP = N_Q_HEADS // N_KV_HEADS
ATTN_WIDTH = N_Q_HEADS * HEAD_DIM
KV_WIDTH = N_KV_HEADS * HEAD_DIM
CONV_WIDTH = 2048
GRID_W = 64
ROPE_THETA = 10000.0
ROPE_AXIS_DIM = HEAD_DIM // 2

LANE = 128
BF16_SUBLANES = 16
FFN_TN = 256
VMEM_LIMIT_BYTES = 56 * 1024 * 1024


def _cparams(sem):
    return pltpu.CompilerParams(dimension_semantics=sem, vmem_limit_bytes=VMEM_LIMIT_BYTES)


def _rms(x):
    return x * lax.rsqrt(jnp.mean(x * x, axis=-1, keepdims=True) + NORM_EPS)


def _norm_kernel(x_ref, g_ref, o_ref):
    o_ref[...] = (_rms(x_ref[...]) * g_ref[...]).astype(o_ref.dtype)


def rms_norm_rows(x, g, tm=256):
    t, d = x.shape
    tm = min(tm, t)
    return pl.pallas_call(
        _norm_kernel,
        grid=(t // tm,),
        in_specs=[pl.BlockSpec((tm, d), lambda i: (i, 0)),
                  pl.BlockSpec((1, d), lambda i: (0, 0))],
        out_specs=pl.BlockSpec((tm, d), lambda i: (i, 0)),
        out_shape=jax.ShapeDtypeStruct((t, d), BF16),
        compiler_params=_cparams(("parallel",)),
        name="rms_norm_rows",
    )(x, g.reshape(1, d))


def _resid_norm_kernel(y_ref, x_ref, gp_ref, gn_ref, xo_ref, ho_ref, *, scale):
    x_new = x_ref[...] + scale * (_rms(y_ref[...]) * gp_ref[...])
    xo_ref[...] = x_new
    ho_ref[...] = (_rms(x_new) * gn_ref[...]).astype(ho_ref.dtype)


def _resid_kernel(y_ref, x_ref, gp_ref, xo_ref, *, scale):
    xo_ref[...] = x_ref[...] + scale * (_rms(y_ref[...]) * gp_ref[...])


def resid_norm_rows(y, x, g_post, scale, g_next=None, tm=256):
    t, d = x.shape
    tm = min(tm, t)
    row = pl.BlockSpec((tm, d), lambda i: (i, 0))
    vec = pl.BlockSpec((1, d), lambda i: (0, 0))
    if g_next is None:
        return pl.pallas_call(
            functools.partial(_resid_kernel, scale=scale),
            grid=(t // tm,),
            in_specs=[row, row, vec],
            out_specs=row,
            out_shape=jax.ShapeDtypeStruct((t, d), F32),
            compiler_params=_cparams(("parallel",)),
            name="resid_rows",
        )(y, x, g_post.reshape(1, d)), None
    return pl.pallas_call(
        functools.partial(_resid_norm_kernel, scale=scale),
        grid=(t // tm,),
        in_specs=[row, row, vec, vec],
        out_specs=[row, row],
        out_shape=[jax.ShapeDtypeStruct((t, d), F32), jax.ShapeDtypeStruct((t, d), BF16)],
        compiler_params=_cparams(("parallel",)),
        name="resid_norm_rows",
    )(y, x, g_post.reshape(1, d), g_next.reshape(1, d))


def _gateup_kernel(a_ref, wg_ref, wu_ref, o_ref):
    a = a_ref[...]
    g = jnp.dot(a, wg_ref[...], preferred_element_type=F32)
    u = jnp.dot(a, wu_ref[...], preferred_element_type=F32)
    o_ref[...] = (g * jax.nn.sigmoid(g) * u).astype(o_ref.dtype)


def gate_up(a, wg, wu, tm=1024, tn=512):
    t, d = a.shape
    f = wg.shape[1]
    tm, tn = min(tm, t), min(tn, f)
    return pl.pallas_call(
        _gateup_kernel,
        grid=(t // tm, f // tn),
        in_specs=[pl.BlockSpec((tm, d), lambda i, j: (i, 0)),
                  pl.BlockSpec((d, tn), lambda i, j: (0, j)),
                  pl.BlockSpec((d, tn), lambda i, j: (0, j))],
        out_specs=pl.BlockSpec((tm, tn), lambda i, j: (i, j)),
        out_shape=jax.ShapeDtypeStruct((t, f), BF16),
        compiler_params=_cparams(("parallel", "arbitrary")),
        name="gate_up",
    )(a, wg, wu)


def _mm_kernel(a_ref, w_ref, o_ref, acc_ref):
    k = pl.program_id(2)
    part = jnp.dot(a_ref[...], w_ref[...], preferred_element_type=F32)

    @pl.when(k == 0)
    def _():
        acc_ref[...] = part

    @pl.when(k > 0)
    def _():
        acc_ref[...] += part

    @pl.when(k == pl.num_programs(2) - 1)
    def _():
        o_ref[...] = acc_ref[...]


def _mm_fullk_kernel(a_ref, w_ref, o_ref):
    o_ref[...] = jnp.dot(a_ref[...], w_ref[...], preferred_element_type=F32)


def matmul(a, w, tm=1024, tn=1024, tk=None):
    t, kd = a.shape
    n = w.shape[1]
    tm, tn = min(tm, t), min(tn, n)
    if tk is None or tk >= kd:
        return pl.pallas_call(
            _mm_fullk_kernel,
            grid=(t // tm, n // tn),
            in_specs=[pl.BlockSpec((tm, kd), lambda i, j: (i, 0)),
                      pl.BlockSpec((kd, tn), lambda i, j: (0, j))],
            out_specs=pl.BlockSpec((tm, tn), lambda i, j: (i, j)),
            out_shape=jax.ShapeDtypeStruct((t, n), F32),
            compiler_params=_cparams(("parallel", "arbitrary")),
            name="matmul_fullk",
        )(a, w)
    return pl.pallas_call(
        _mm_kernel,
        grid=(t // tm, n // tn, kd // tk),
        in_specs=[pl.BlockSpec((tm, tk), lambda i, j, k: (i, k)),
                  pl.BlockSpec((tk, tn), lambda i, j, k: (k, j))],
        out_specs=pl.BlockSpec((tm, tn), lambda i, j, k: (i, j)),
        out_shape=jax.ShapeDtypeStruct((t, n), F32),
        scratch_shapes=[pltpu.VMEM((tm, tn), F32)],
        compiler_params=_cparams(("parallel", "arbitrary", "arbitrary")),
        name="matmul_kgrid",
    )(a, w)


QKV_TN = KV_GROUP * HEAD_DIM
N_Q_TILES = ATTN_WIDTH // QKV_TN
ATTN_TQ = 256
ATTN_TK = 1024
ATTN_SLAB = 64
ATTN_CHUNK_UNROLL = 4
SCORE_BOUND_LOG2 = 40.0
Q_SCALE = (HEAD_DIM ** -0.5) * 1.4426950408889634


def _inproj_qkv_kernel(a_ref, w_ref, qg_ref, kg_ref, cos_ref, sin_ref, qt_ref, k_ref, vt_ref):
    j = pl.program_id(1)
    acc = jnp.dot(a_ref[...], w_ref[...], preferred_element_type=F32)
    tm = acc.shape[0]

    def head(h):
        return acc[:, h * HEAD_DIM:(h + 1) * HEAD_DIM]

    def norm_rope(y, gain):
        y = _rms(y) * gain
        return y * cos_ref[...] + pltpu.roll(y, ROPE_AXIS_DIM, axis=1) * sin_ref[...]

    def store_transposed(dst_ref, h, y, width):
        yt = y.T.astype(dst_ref.dtype)
        for c in range(tm // width):
            dst_ref[h, c] = yt[:, c * width:(c + 1) * width]

    @pl.when(j < N_Q_TILES)
    def _():
        for h in range(KV_GROUP):
            store_transposed(qt_ref, h, norm_rope(head(h), qg_ref[...]) * Q_SCALE, ATTN_TQ)

    @pl.when(j == N_Q_TILES)
    def _():
        for h in range(KV_GROUP):
            k_ref[h] = norm_rope(head(h), kg_ref[...]).astype(k_ref.dtype)

    @pl.when(j == N_Q_TILES + 1)
    def _():
        for h in range(KV_GROUP):
            store_transposed(vt_ref, h, head(h), ATTN_TK)


def inproj_qkv(a, w_in, q_norm, k_norm, cos_full, sin_signed, seq_len, tm=1024):
    t, d = a.shape
    tm = min(tm, seq_len)
    n_seq_tiles = seq_len // tm
    return pl.pallas_call(
        _inproj_qkv_kernel,
        grid=(t // tm, N_Q_TILES + 2),
        in_specs=[pl.BlockSpec((tm, d), lambda i, j: (i, 0)),
                  pl.BlockSpec((d, QKV_TN), lambda i, j: (0, j)),
                  pl.BlockSpec((1, HEAD_DIM), lambda i, j: (0, 0)),
                  pl.BlockSpec((1, HEAD_DIM), lambda i, j: (0, 0)),
                  pl.BlockSpec((tm, HEAD_DIM), lambda i, j: (i % n_seq_tiles, 0)),
                  pl.BlockSpec((tm, HEAD_DIM), lambda i, j: (i % n_seq_tiles, 0))],
        out_specs=[pl.BlockSpec((KV_GROUP, tm // ATTN_TQ, HEAD_DIM, ATTN_TQ),
                                lambda i, j: (jnp.minimum(j, N_Q_TILES - 1), i, 0, 0)),
                   pl.BlockSpec((N_KV_HEADS, tm, HEAD_DIM), lambda i, j: (0, i, 0)),
                   pl.BlockSpec((N_KV_HEADS, tm // ATTN_TK, HEAD_DIM, ATTN_TK), lambda i, j: (0, i, 0, 0))],
        out_shape=[jax.ShapeDtypeStruct((N_Q_HEADS, t // ATTN_TQ, HEAD_DIM, ATTN_TQ), BF16),
                   jax.ShapeDtypeStruct((N_KV_HEADS, t, HEAD_DIM), BF16),
                   jax.ShapeDtypeStruct((N_KV_HEADS, t // ATTN_TK, HEAD_DIM, ATTN_TK), BF16)],
        compiler_params=_cparams(("parallel", "arbitrary")),
        name="inproj_qkv",
    )(a, w_in, q_norm.reshape(1, HEAD_DIM), k_norm.reshape(1, HEAD_DIM), cos_full, sin_signed)


def _inproj_conv_kernel(a_ref, wb_ref, wc_ref, wh_ref, gb_ref, u_ref):
    a = a_ref[...]
    gb_ref[...] = jnp.dot(a, wb_ref[...], preferred_element_type=F32).astype(gb_ref.dtype)
    c = jnp.dot(a, wc_ref[...], preferred_element_type=F32)
    h = jnp.dot(a, wh_ref[...], preferred_element_type=F32)
    u_ref[...] = (c * h).astype(u_ref.dtype)


def inproj_conv(a, w_in, tm=1024, tn=512):
    t, d = a.shape
    tm = min(tm, t)
    off_b = (ATTN_WIDTH + 2 * KV_WIDTH) // tn
    off_c = off_b + CONV_WIDTH // tn
    off_h = off_c + CONV_WIDTH // tn
    out = jax.ShapeDtypeStruct((t, CONV_WIDTH), BF16)
    return pl.pallas_call(
        _inproj_conv_kernel,
        grid=(t // tm, CONV_WIDTH // tn),
        in_specs=[pl.BlockSpec((tm, d), lambda i, j: (i, 0)),
                  pl.BlockSpec((d, tn), lambda i, j: (0, off_b + j)),
                  pl.BlockSpec((d, tn), lambda i, j: (0, off_c + j)),
                  pl.BlockSpec((d, tn), lambda i, j: (0, off_h + j))],
        out_specs=[pl.BlockSpec((tm, tn), lambda i, j: (i, j)),
                   pl.BlockSpec((tm, tn), lambda i, j: (i, j))],
        out_shape=[out, out],
        compiler_params=_cparams(("parallel", "arbitrary")),
        name="inproj_conv",
    )(a, w_in, w_in, w_in)


def _attn_kernel(qt_ref, k_ref, vt_ref, o_ref, ksq_ref, m_ref, l_ref, acc_ref, s_ref, p_ref, alpha_ref):
    n_heads = qt_ref.shape[0]
    n_chunks = vt_ref.shape[1]
    tk = vt_ref.shape[3]
    slabs = [pl.ds(r, ATTN_SLAB) for r in range(0, tk, ATTN_SLAB)]

    def key_chunk(c):
        return k_ref[0, pl.ds(pl.multiple_of(c * tk, tk), tk), :]

    @pl.when(pl.program_id(2) == 0)
    def _():
        def chunk_max(c, best):
            kc = key_chunk(c).astype(F32)
            return jnp.maximum(best, jnp.max(jnp.sum(kc * kc, axis=1, keepdims=True)))
        ksq_ref[0] = lax.fori_loop(0, n_chunks, chunk_max, jnp.float32(0.0))

    qsq = jnp.float32(0.0)
    for h in range(n_heads):
        q = qt_ref[h, 0].astype(F32)
        qsq = jnp.maximum(qsq, jnp.max(jnp.sum(q * q, axis=0, keepdims=True)))
    scores_bounded = qsq * ksq_ref[0] <= SCORE_BOUND_LOG2 * SCORE_BOUND_LOG2

    l_ref[...] = jnp.zeros(l_ref.shape, F32)
    acc_ref[...] = jnp.zeros(acc_ref.shape, F32)
    p_ref[...] = jnp.zeros(p_ref.shape, p_ref.dtype)

    def pipeline(stage_scores, stage_values, unroll):
        unroll = unroll if n_chunks % unroll == 0 else 1

        @pl.loop(0, n_chunks // unroll)
        def _(step):
            for sub in range(unroll):
                c = step * unroll + sub
                for h in range(n_heads):
                    slot, other = h % 2, (h + 1) % 2
                    stage_scores(c, h, slot)
                    if h > 0:
                        stage_values(c, h - 1, other)
                    else:
                        stage_values(jnp.maximum(c - 1, 0), n_heads - 1, other)
        stage_values(n_chunks - 1, n_heads - 1, (n_heads - 1) % 2)

    def unshifted():
        def probs(c, h, slot):
            s = jnp.dot(key_chunk(c), qt_ref[h, 0], preferred_element_type=F32)
            slab_sum = jnp.zeros((ATTN_SLAB, s.shape[1]), F32)
            for r in range(0, tk, ATTN_SLAB):
                p = jnp.exp2(s[r:r + ATTN_SLAB])
                slab_sum = slab_sum + p
                p_ref[slot, pl.ds(r, ATTN_SLAB), :] = p.astype(p_ref.dtype)
            l_ref[h] = l_ref[h] + jnp.sum(slab_sum, axis=0, keepdims=True)

        def values(c, h, slot):
            acc_ref[h] = acc_ref[h] + jnp.dot(vt_ref[0, c], p_ref[slot], preferred_element_type=F32)

        pipeline(probs, values, ATTN_CHUNK_UNROLL)

    def running_max():
        def scores(c, h, slot):
            s_ref[slot] = jnp.dot(key_chunk(c), qt_ref[h, 0], preferred_element_type=F32)

        def softmax(h, slot):
            m_old = m_ref[h]
            slab_max = s_ref[slot, slabs[0], :]
            for rows in slabs[1:]:
                slab_max = jnp.maximum(slab_max, s_ref[slot, rows, :])
            m_new = jnp.maximum(m_old, jnp.max(slab_max, axis=0, keepdims=True))
            alpha = jnp.exp2(m_old - m_new)
            slab_sum = jnp.zeros(slab_max.shape, F32)
            for rows in slabs:
                p = jnp.exp2(s_ref[slot, rows, :] - m_new)
                slab_sum = slab_sum + p
                p_ref[slot, rows, :] = p.astype(p_ref.dtype)
            l_ref[h] = alpha * l_ref[h] + jnp.sum(slab_sum, axis=0, keepdims=True)
            m_ref[h] = m_new
            alpha_ref[slot] = alpha

        def scores_and_softmax(c, h, slot):
            if h + 1 < n_heads:
                scores(c, h + 1, (h + 1) % 2)
            else:
                scores(jnp.minimum(c + 1, n_chunks - 1), 0, (h + 1) % 2)
            softmax(h, slot)

        def values(c, h, slot):
            pv = jnp.dot(vt_ref[0, c], p_ref[slot], preferred_element_type=F32)
            acc_ref[h] = alpha_ref[slot] * acc_ref[h] + pv

        m_ref[...] = jnp.full(m_ref.shape, -jnp.inf, F32)
        alpha_ref[...] = jnp.ones(alpha_ref.shape, F32)
        scores(0, 0, 0)
        pipeline(scores_and_softmax, values, 1)

    lax.cond(scores_bounded, unshifted, running_max)

    for h in range(n_heads):
        out_t = acc_ref[h] / l_ref[h]
        o_ref[:, h * HEAD_DIM:(h + 1) * HEAD_DIM] = out_t.T.astype(o_ref.dtype)


def attention(qt, k, vt, batch, seq_len):
    t = k.shape[1]
    nq = seq_len // ATTN_TQ
    nk = seq_len // ATTN_TK
    return pl.pallas_call(
        _attn_kernel,
        grid=(batch, N_KV_HEADS, nq),
        in_specs=[pl.BlockSpec((KV_GROUP, 1, HEAD_DIM, ATTN_TQ), lambda b, g, i: (g, b * nq + i, 0, 0)),
                  pl.BlockSpec((1, seq_len, HEAD_DIM), lambda b, g, i: (g, b, 0)),
                  pl.BlockSpec((1, nk, HEAD_DIM, ATTN_TK), lambda b, g, i: (g, b, 0, 0))],
        out_specs=pl.BlockSpec((ATTN_TQ, KV_GROUP * HEAD_DIM), lambda b, g, i: (b * nq + i, g)),
        out_shape=jax.ShapeDtypeStruct((t, ATTN_WIDTH), BF16),
        scratch_shapes=[pltpu.SMEM((1,), F32),
                        pltpu.VMEM((KV_GROUP, 1, ATTN_TQ), F32),
                        pltpu.VMEM((KV_GROUP, 1, ATTN_TQ), F32),
                        pltpu.VMEM((KV_GROUP, HEAD_DIM, ATTN_TQ), F32),
                        pltpu.VMEM((2, ATTN_TK, ATTN_TQ), F32),
                        pltpu.VMEM((2, ATTN_TK, ATTN_TQ), BF16),
                        pltpu.VMEM((2, 1, ATTN_TQ), F32)],
        compiler_params=_cparams(("parallel", "parallel", "arbitrary")),
        name="attention",
    )(qt, k, vt)


def _premix_kernel(attn_ref, gb_ref, u_ref, up_ref, un_ref, cw_ref, ga_ref, gc_ref, o_ref, *, n_seq_tiles):
    i = pl.program_id(0)
    tm = u_ref.shape[0]
    o_ref[:, :ATTN_WIDTH] = (_rms(attn_ref[...].astype(F32)) * ga_ref[...]).astype(o_ref.dtype)

    u = u_ref[...].astype(F32)
    first = (i % n_seq_tiles) == 0
    last = (i % n_seq_tiles) == n_seq_tiles - 1
    prev_row = jnp.where(first, 0.0, up_ref[BF16_SUBLANES - 1:BF16_SUBLANES, :].astype(F32))
    next_row = jnp.where(last, 0.0, un_ref[0:1, :].astype(F32))
    row = lax.broadcasted_iota(jnp.int32, u.shape, 0)
    u_prev = jnp.where(row == 0, prev_row, pltpu.roll(u, 1, axis=0))
    u_next = jnp.where(row == tm - 1, next_row, pltpu.roll(u, tm - 1, axis=0))
    cw = cw_ref[...]
    conv = gb_ref[...].astype(F32) * (cw[0:1] * u_prev + cw[1:2] * u + cw[2:3] * u_next)
    o_ref[:, ATTN_WIDTH:] = (_rms(conv) * gc_ref[...]).astype(o_ref.dtype)


def premix(attn, gate_b, u, conv_w, g_attn, g_conv, seq_len, tm=256):
    t = attn.shape[0]
    tm = min(tm, seq_len)
    n_seq_tiles = seq_len // tm
    halo = tm // BF16_SUBLANES
    n_halo = t // BF16_SUBLANES
    row = lambda w: pl.BlockSpec((tm, w), lambda i: (i, 0))
    vec = lambda w: pl.BlockSpec((1, w), lambda i: (0, 0))
    return pl.pallas_call(
        functools.partial(_premix_kernel, n_seq_tiles=n_seq_tiles),
        grid=(t // tm,),
        in_specs=[row(ATTN_WIDTH), row(CONV_WIDTH), row(CONV_WIDTH),
                  pl.BlockSpec((BF16_SUBLANES, CONV_WIDTH), lambda i: (jnp.maximum(i * halo - 1, 0), 0)),
                  pl.BlockSpec((BF16_SUBLANES, CONV_WIDTH),
                               lambda i: (jnp.minimum((i + 1) * halo, n_halo - 1), 0)),
                  pl.BlockSpec((3, CONV_WIDTH), lambda i: (0, 0)),
                  vec(ATTN_WIDTH), vec(CONV_WIDTH)],
        out_specs=row(ATTN_WIDTH + CONV_WIDTH),
        out_shape=jax.ShapeDtypeStruct((t, ATTN_WIDTH + CONV_WIDTH), BF16),
        compiler_params=_cparams(("parallel",)),
        name="premix",
    )(attn, gate_b, u, u, u, conv_w, g_attn.reshape(1, -1), g_conv.reshape(1, -1))


def _rope_tables(seq_len):
    rows = seq_len // GRID_W
    inv = 1.0 / (ROPE_THETA ** (jnp.arange(0, ROPE_AXIS_DIM, 2, dtype=F32) / ROPE_AXIS_DIM))
    row_ang = jnp.arange(rows, dtype=F32)[:, None] * inv
    col_ang = jnp.arange(GRID_W, dtype=F32)[:, None] * inv
    ang = jnp.concatenate([
        jnp.broadcast_to(row_ang[:, None, :], (rows, GRID_W, inv.shape[0])),
        jnp.broadcast_to(col_ang[None, :, :], (rows, GRID_W, inv.shape[0])),
    ], axis=-1).reshape(seq_len, ROPE_AXIS_DIM)
    cos, sin = jnp.cos(ang), jnp.sin(ang)
    return jnp.concatenate([cos, cos], axis=-1), jnp.concatenate([-sin, sin], axis=-1)


def _prep_ffn(w_gate, w_up, w_down):
    return w_gate.astype(BF16), w_up.astype(BF16), w_down.astype(BF16)


def _ffn(hn, x, w, g_post, g_next):
    wg, wu, wd = w
    h = gate_up(hn, wg, wu, tm=2048, tn=FFN_TN)
    y = matmul(h, wd, tm=512, tn=512)
    return resid_norm_rows(y, x, g_post, FFN_RES_SCALE, g_next)


def _trunk(x3d, p):
    b, s, d = x3d.shape
    x = x3d.reshape(b * s, d)
    cos_full, sin_signed = _rope_tables(s)
    hn = rms_norm_rows(x, p["ffn1_pre"])
    x, hn = _ffn(hn, x, p["ffn1"], p["ffn1_post"], p["mix_pre"])
    qt, k, vt = inproj_qkv(hn, p["w_in"], p["q_norm"], p["k_norm"], cos_full, sin_signed, s)
    gate_b, u = inproj_conv(hn, p["w_in"])
    attn = attention(qt, k, vt, b, s)
    mix_in = premix(attn, gate_b, u, p["conv_w"], p["attn_out_norm"], p["conv_out_norm"], s)
    y = matmul(mix_in, p["w_out"])
    x, hn = resid_norm_rows(y, x, p["mix_post"], 1.0, p["ffn2_pre"])
    x, _ = _ffn(hn, x, p["ffn2"], p["ffn2_post"], None)
    return x.reshape(b, s, d)


def kernel(x_prompt, x_sample, ffn1_pre, ffn1_post, w1_gate, w1_up, w1_down, mix_pre, mix_post, w_in, q_norm, k_norm, conv_w, attn_out_norm, conv_out_norm, w_out, ffn2_pre, ffn2_post, w2_gate, w2_up, w2_down):
    p = {
        "ffn1_pre": ffn1_pre[0], "ffn1_post": ffn1_post[0],
        "ffn1": _prep_ffn(w1_gate[0], w1_up[0], w1_down[0]),
        "mix_pre": mix_pre[0], "mix_post": mix_post[0],
        "w_in": w_in[0].astype(BF16), "q_norm": q_norm[0], "k_norm": k_norm[0],
        "conv_w": conv_w[0], "attn_out_norm": attn_out_norm[0], "conv_out_norm": conv_out_norm[0],
        "w_out": w_out[0].astype(BF16),
        "ffn2_pre": ffn2_pre[0], "ffn2_post": ffn2_post[0],
        "ffn2": _prep_ffn(w2_gate[0], w2_up[0], w2_down[0]),
    }
    return _trunk(x_prompt, p), _trunk(x_sample, p)
```

```python
import functools

import jax
import jax.numpy as jnp
from jax import lax
from jax.experimental import pallas as pl
from jax.experimental.pallas import tpu as pltpu

F32 = jnp.float32
BF16 = jnp.bfloat16

NORM_EPS = 1e-6
FFN_RES_SCALE = 0.5
HEAD_DIM = 128
N_Q_HEADS = 16
N_KV_HEADS = 4
KV_GROUP = N_Q_HEADS // N_KV_HEADS
ATTN_WIDTH = N_Q_HEADS * HEAD_DIM
KV_WIDTH = N_KV_HEADS * HEAD_DIM
CONV_WIDTH = 2048
GRID_W = 64
ROPE_THETA = 10000.0
ROPE_AXIS_DIM = HEAD_DIM // 2

LANE = 128
BF16_SUBLANES = 16
FFN_TN = 256
VMEM_LIMIT_BYTES = 56 * 1024 * 1024


def _cparams(sem):
    return pltpu.CompilerParams(dimension_semantics=sem, vmem_limit_bytes=VMEM_LIMIT_BYTES)


def _rms(x):
    return x * lax.rsqrt(jnp.mean(x * x, axis=-1, keepdims=True) + NORM_EPS)


def _norm_kernel(x_ref, g_ref, o_ref):
    o_ref[...] = (_rms(x_ref[...]) * g_ref[...]).astype(o_ref.dtype)


def rms_norm_rows(x, g, tm=256):
    t, d = x.shape
    tm = min(tm, t)
    return pl.pallas_call(
        _norm_kernel,
        grid=(t // tm,),
        in_specs=[pl.BlockSpec((tm, d), lambda i: (i, 0)),
                  pl.BlockSpec((1, d), lambda i: (0, 0))],
        out_specs=pl.BlockSpec((tm, d), lambda i: (i, 0)),
        out_shape=jax.ShapeDtypeStruct((t, d), BF16),
        compiler_params=_cparams(("parallel",)),
        name="rms_norm_rows",
    )(x, g.reshape(1, d))


def _resid_norm_kernel(y_ref, x_ref, gp_ref, gn_ref, xo_ref, ho_ref, *, scale):
    x_new = x_ref[...] + scale * (_rms(y_ref[...]) * gp_ref[...])
    xo_ref[...] = x_new
    ho_ref[...] = (_rms(x_new) * gn_ref[...]).astype(ho_ref.dtype)


def _resid_kernel(y_ref, x_ref, gp_ref, xo_ref, *, scale):
    xo_ref[...] = x_ref[...] + scale * (_rms(y_ref[...]) * gp_ref[...])


def resid_norm_rows(y, x, g_post, scale, g_next=None, tm=256):
    t, d = x.shape
    tm = min(tm, t)
    row = pl.BlockSpec((tm, d), lambda i: (i, 0))
    vec = pl.BlockSpec((1, d), lambda i: (0, 0))
    if g_next is None:
        return pl.pallas_call(
            functools.partial(_resid_kernel, scale=scale),
            grid=(t // tm,),
            in_specs=[row, row, vec],
            out_specs=row,
            out_shape=jax.ShapeDtypeStruct((t, d), F32),
            compiler_params=_cparams(("parallel",)),
            name="resid_rows",
        )(y, x, g_post.reshape(1, d)), None
    return pl.pallas_call(
        functools.partial(_resid_norm_kernel, scale=scale),
        grid=(t // tm,),
        in_specs=[row, row, vec, vec],
        out_specs=[row, row],
        out_shape=[jax.ShapeDtypeStruct((t, d), F32), jax.ShapeDtypeStruct((t, d), BF16)],
        compiler_params=_cparams(("parallel",)),
        name="resid_norm_rows",
    )(y, x, g_post.reshape(1, d), g_next.reshape(1, d))


def _gateup_kernel(a_ref, wg_ref, wu_ref, o_ref):
    a = a_ref[...]
    g = jnp.dot(a, wg_ref[...], preferred_element_type=F32)
    u = jnp.dot(a, wu_ref[...], preferred_element_type=F32)
    o_ref[...] = (g * jax.nn.sigmoid(g) * u).astype(o_ref.dtype)


def gate_up(a, wg, wu, tm=1024, tn=512):
    t, d = a.shape
    f = wg.shape[1]
    tm, tn = min(tm, t), min(tn, f)
    return pl.pallas_call(
        _gateup_kernel,
        grid=(t // tm, f // tn),
        in_specs=[pl.BlockSpec((tm, d), lambda i, j: (i, 0), pipeline_mode=pl.Buffered(1)),
                  pl.BlockSpec((d, tn), lambda i, j: (0, j)),
                  pl.BlockSpec((d, tn), lambda i, j: (0, j))],
        out_specs=pl.BlockSpec((tm, tn), lambda i, j: (i, j)),
        out_shape=jax.ShapeDtypeStruct((t, f), BF16),
        compiler_params=_cparams(("parallel", "arbitrary")),
        name="gate_up",
    )(a, wg, wu)


def _mm_kernel(a_ref, w_ref, o_ref, acc_ref):
    k = pl.program_id(2)
    part = jnp.dot(a_ref[...], w_ref[...], preferred_element_type=F32)

    @pl.when(k == 0)
    def _():
        acc_ref[...] = part

    @pl.when(k > 0)
    def _():
        acc_ref[...] += part

    @pl.when(k == pl.num_programs(2) - 1)
    def _():
        o_ref[...] = acc_ref[...]


def _mm_fullk_kernel(a_ref, w_ref, o_ref):
    o_ref[...] = jnp.dot(a_ref[...], w_ref[...], preferred_element_type=F32)


def matmul(a, w, tm=1024, tn=1024, tk=None):
    t, kd = a.shape
    n = w.shape[1]
    tm, tn = min(tm, t), min(tn, n)
    if tk is None or tk >= kd:
        return pl.pallas_call(
            _mm_fullk_kernel,
            grid=(t // tm, n // tn),
            in_specs=[pl.BlockSpec((tm, kd), lambda i, j: (i, 0)),
                      pl.BlockSpec((kd, tn), lambda i, j: (0, j))],
            out_specs=pl.BlockSpec((tm, tn), lambda i, j: (i, j)),
            out_shape=jax.ShapeDtypeStruct((t, n), F32),
            compiler_params=_cparams(("parallel", "arbitrary")),
            name="matmul_fullk",
        )(a, w)
    return pl.pallas_call(
        _mm_kernel,
        grid=(t // tm, n // tn, kd // tk),
        in_specs=[pl.BlockSpec((tm, tk), lambda i, j, k: (i, k)),
                  pl.BlockSpec((tk, tn), lambda i, j, k: (k, j))],
        out_specs=pl.BlockSpec((tm, tn), lambda i, j, k: (i, j)),
        out_shape=jax.ShapeDtypeStruct((t, n), F32),
        scratch_shapes=[pltpu.VMEM((tm, tn), F32)],
        compiler_params=_cparams(("parallel", "arbitrary", "arbitrary")),
        name="matmul_kgrid",
    )(a, w)


QKV_TN = KV_GROUP * HEAD_DIM
N_Q_TILES = ATTN_WIDTH // QKV_TN
ATTN_TQ = 256
ATTN_TK = 1024
ATTN_SLAB = 64
ATTN_Q_TILES = 2
ATTN_CHUNK_UNROLL = 2
SCORE_BOUND_LOG2 = 40.0
Q_SCALE = (HEAD_DIM ** -0.5) * 1.4426950408889634


def _inproj_qkv_kernel(a_ref, w_ref, qg_ref, kg_ref, cos_ref, sin_ref, qt_ref, k_ref, vt_ref):
    j = pl.program_id(1)
    acc = jnp.dot(a_ref[...], w_ref[...], preferred_element_type=F32)
    tm = acc.shape[0]

    def head(h):
        return acc[:, h * HEAD_DIM:(h + 1) * HEAD_DIM]

    def norm_rope(y, gain):
        y = _rms(y) * gain
        return y * cos_ref[...] + pltpu.roll(y, ROPE_AXIS_DIM, axis=1) * sin_ref[...]

    def store_transposed(dst_ref, h, y, width):
        yt = y.T.astype(dst_ref.dtype)
        for c in range(tm // width):
            dst_ref[h, c] = yt[:, c * width:(c + 1) * width]

    @pl.when(j < N_Q_TILES)
    def _():
        for h in range(KV_GROUP):
            store_transposed(qt_ref, h, norm_rope(head(h), qg_ref[...]) * Q_SCALE, ATTN_TQ)

    @pl.when(j == N_Q_TILES)
    def _():
        for h in range(KV_GROUP):
            k_ref[h] = norm_rope(head(h), kg_ref[...]).astype(k_ref.dtype)

    @pl.when(j == N_Q_TILES + 1)
    def _():
        for h in range(KV_GROUP):
            store_transposed(vt_ref, h, head(h), ATTN_TK)


def inproj_qkv(a, w_in, q_norm, k_norm, cos_full, sin_signed, seq_len, tm=1024):
    t, d = a.shape
    tm = min(tm, seq_len)
    n_seq_tiles = seq_len // tm
    return pl.pallas_call(
        _inproj_qkv_kernel,
        grid=(t // tm, N_Q_TILES + 2),
        in_specs=[pl.BlockSpec((tm, d), lambda i, j: (i, 0)),
                  pl.BlockSpec((d, QKV_TN), lambda i, j: (0, j)),
                  pl.BlockSpec((1, HEAD_DIM), lambda i, j: (0, 0)),
                  pl.BlockSpec((1, HEAD_DIM), lambda i, j: (0, 0)),
                  pl.BlockSpec((tm, HEAD_DIM), lambda i, j: (i % n_seq_tiles, 0)),
                  pl.BlockSpec((tm, HEAD_DIM), lambda i, j: (i % n_seq_tiles, 0))],
        out_specs=[pl.BlockSpec((KV_GROUP, tm // ATTN_TQ, HEAD_DIM, ATTN_TQ),
                                lambda i, j: (jnp.minimum(j, N_Q_TILES - 1), i, 0, 0)),
                   pl.BlockSpec((N_KV_HEADS, tm, HEAD_DIM), lambda i, j: (0, i, 0)),
                   pl.BlockSpec((N_KV_HEADS, tm // ATTN_TK, HEAD_DIM, ATTN_TK), lambda i, j: (0, i, 0, 0))],
        out_shape=[jax.ShapeDtypeStruct((N_Q_HEADS, t // ATTN_TQ, HEAD_DIM, ATTN_TQ), BF16),
                   jax.ShapeDtypeStruct((N_KV_HEADS, t, HEAD_DIM), BF16),
                   jax.ShapeDtypeStruct((N_KV_HEADS, t // ATTN_TK, HEAD_DIM, ATTN_TK), BF16)],
        compiler_params=_cparams(("parallel", "arbitrary")),
        name="inproj_qkv",
    )(a, w_in, q_norm.reshape(1, HEAD_DIM), k_norm.reshape(1, HEAD_DIM), cos_full, sin_signed)


def _inproj_conv_kernel(a_ref, wb_ref, wc_ref, wh_ref, gb_ref, u_ref):
    a = a_ref[...]
    gb_ref[...] = jnp.dot(a, wb_ref[...], preferred_element_type=F32).astype(gb_ref.dtype)
    c = jnp.dot(a, wc_ref[...], preferred_element_type=F32)
    h = jnp.dot(a, wh_ref[...], preferred_element_type=F32)
    u_ref[...] = (c * h).astype(u_ref.dtype)


def inproj_conv(a, w_in, tm=1024, tn=512):
    t, d = a.shape
    tm = min(tm, t)
    off_b = (ATTN_WIDTH + 2 * KV_WIDTH) // tn
    off_c = off_b + CONV_WIDTH // tn
    off_h = off_c + CONV_WIDTH // tn
    out = jax.ShapeDtypeStruct((t, CONV_WIDTH), BF16)
    return pl.pallas_call(
        _inproj_conv_kernel,
        grid=(t // tm, CONV_WIDTH // tn),
        in_specs=[pl.BlockSpec((tm, d), lambda i, j: (i, 0)),
                  pl.BlockSpec((d, tn), lambda i, j: (0, off_b + j)),
                  pl.BlockSpec((d, tn), lambda i, j: (0, off_c + j)),
                  pl.BlockSpec((d, tn), lambda i, j: (0, off_h + j))],
        out_specs=[pl.BlockSpec((tm, tn), lambda i, j: (i, j)),
                   pl.BlockSpec((tm, tn), lambda i, j: (i, j))],
        out_shape=[out, out],
        compiler_params=_cparams(("parallel", "arbitrary")),
        name="inproj_conv",
    )(a, w_in, w_in, w_in)


def _attn_kernel(qt_ref, k_ref, vt_ref, o_ref, ksq_ref, m_ref, l_ref, acc_ref, s_ref, p_ref, alpha_ref):
    group, n_q_tiles, _, tq = qt_ref.shape
    n_heads = group * n_q_tiles
    n_chunks = vt_ref.shape[1]
    tk = vt_ref.shape[3]
    slabs = [pl.ds(r, ATTN_SLAB) for r in range(0, tk, ATTN_SLAB)]

    def q_tile(h):
        return qt_ref[h % group, h // group]

    def key_chunk(c):
        return k_ref[0, pl.ds(pl.multiple_of(c * tk, tk), tk), :]

    @pl.when(pl.program_id(2) == 0)
    def _():
        def chunk_max(c, best):
            kc = key_chunk(c).astype(F32)
            return jnp.maximum(best, jnp.max(jnp.sum(kc * kc, axis=1, keepdims=True)))
        ksq_ref[0] = lax.fori_loop(0, n_chunks, chunk_max, jnp.float32(0.0))

    qsq = jnp.float32(0.0)
    for h in range(n_heads):
        q = q_tile(h).astype(F32)
        qsq = jnp.maximum(qsq, jnp.max(jnp.sum(q * q, axis=0, keepdims=True)))
    scores_bounded = qsq * ksq_ref[0] <= SCORE_BOUND_LOG2 * SCORE_BOUND_LOG2

    l_ref[...] = jnp.zeros(l_ref.shape, F32)
    acc_ref[...] = jnp.zeros(acc_ref.shape, F32)
    p_ref[...] = jnp.zeros(p_ref.shape, p_ref.dtype)

    def pipeline(stage_scores, stage_values, unroll):
        unroll = unroll if n_chunks % unroll == 0 else 1

        @pl.loop(0, n_chunks // unroll)
        def _(step):
            for sub in range(unroll):
                c = step * unroll + sub
                for h in range(n_heads):
                    slot, other = h % 2, (h + 1) % 2
                    stage_scores(c, h, slot)
                    if h > 0:
                        stage_values(c, h - 1, other)
                    else:
                        stage_values(jnp.maximum(c - 1, 0), n_heads - 1, other)
        stage_values(n_chunks - 1, n_heads - 1, (n_heads - 1) % 2)

    def unshifted():
        def probs(c, h, slot):
            s = jnp.dot(key_chunk(c), q_tile(h), preferred_element_type=F32)
            slab_sum = jnp.zeros((ATTN_SLAB, s.shape[1]), F32)
            for r in range(0, tk, ATTN_SLAB):
                p = jnp.exp2(s[r:r + ATTN_SLAB])
                slab_sum = slab_sum + p
                p_ref[slot, pl.ds(r, ATTN_SLAB), :] = p.astype(p_ref.dtype)
            l_ref[h] = l_ref[h] + jnp.sum(slab_sum, axis=0, keepdims=True)

        def values(c, h, slot):
            acc_ref[h] = acc_ref[h] + jnp.dot(vt_ref[0, c], p_ref[slot], preferred_element_type=F32)

        pipeline(probs, values, ATTN_CHUNK_UNROLL)

    def running_max():
        def scores(c, h, slot):
            s_ref[slot] = jnp.dot(key_chunk(c), q_tile(h), preferred_element_type=F32)

        def softmax(h, slot):
            m_old = m_ref[h]
            slab_max = s_ref[slot, slabs[0], :]
            for rows in slabs[1:]:
                slab_max = jnp.maximum(slab_max, s_ref[slot, rows, :])
            m_new = jnp.maximum(m_old, jnp.max(slab_max, axis=0, keepdims=True))
            alpha = jnp.exp2(m_old - m_new)
            slab_sum = jnp.zeros(slab_max.shape, F32)
            for rows in slabs:
                p = jnp.exp2(s_ref[slot, rows, :] - m_new)
                slab_sum = slab_sum + p
                p_ref[slot, rows, :] = p.astype(p_ref.dtype)
            l_ref[h] = alpha * l_ref[h] + jnp.sum(slab_sum, axis=0, keepdims=True)
            m_ref[h] = m_new
            alpha_ref[slot] = alpha

        def scores_and_softmax(c, h, slot):
            if h + 1 < n_heads:
                scores(c, h + 1, (h + 1) % 2)
            else:
                scores(jnp.minimum(c + 1, n_chunks - 1), 0, (h + 1) % 2)
            softmax(h, slot)

        def values(c, h, slot):
            pv = jnp.dot(vt_ref[0, c], p_ref[slot], preferred_element_type=F32)
            acc_ref[h] = alpha_ref[slot] * acc_ref[h] + pv

        m_ref[...] = jnp.full(m_ref.shape, -jnp.inf, F32)
        alpha_ref[...] = jnp.ones(alpha_ref.shape, F32)
        scores(0, 0, 0)
        pipeline(scores_and_softmax, values, 1)

    lax.cond(scores_bounded, unshifted, running_max)

    for h in range(n_heads):
        out_t = acc_ref[h] / l_ref[h]
        head, tile = h % group, h // group
        o_ref[tile * tq:(tile + 1) * tq, head * HEAD_DIM:(head + 1) * HEAD_DIM] = out_t.T.astype(o_ref.dtype)


def attention(qt, k, vt, batch, seq_len):
    t = k.shape[1]
    q_tiles = min(ATTN_Q_TILES, seq_len // ATTN_TQ)
    nq = seq_len // (ATTN_TQ * q_tiles)
    nk = seq_len // ATTN_TK
    n_units = KV_GROUP * q_tiles
    return pl.pallas_call(
        _attn_kernel,
        grid=(batch, N_KV_HEADS, nq),
        in_specs=[pl.BlockSpec((KV_GROUP, q_tiles, HEAD_DIM, ATTN_TQ), lambda b, g, i: (g, b * nq + i, 0, 0)),
                  pl.BlockSpec((1, seq_len, HEAD_DIM), lambda b, g, i: (g, b, 0)),
                  pl.BlockSpec((1, nk, HEAD_DIM, ATTN_TK), lambda b, g, i: (g, b, 0, 0))],
        out_specs=pl.BlockSpec((ATTN_TQ * q_tiles, KV_GROUP * HEAD_DIM), lambda b, g, i: (b * nq + i, g)),
        out_shape=jax.ShapeDtypeStruct((t, ATTN_WIDTH), BF16),
        scratch_shapes=[pltpu.SMEM((1,), F32),
                        pltpu.VMEM((n_units, 1, ATTN_TQ), F32),
                        pltpu.VMEM((n_units, 1, ATTN_TQ), F32),
                        pltpu.VMEM((n_units, HEAD_DIM, ATTN_TQ), F32),
                        pltpu.VMEM((2, ATTN_TK, ATTN_TQ), F32),
                        pltpu.VMEM((2, ATTN_TK, ATTN_TQ), BF16),
                        pltpu.VMEM((2, 1, ATTN_TQ), F32)],
        compiler_params=_cparams(("parallel", "parallel", "arbitrary")),
        name="attention",
    )(qt, k, vt)


def _premix_kernel(attn_ref, gb_ref, u_ref, up_ref, un_ref, cw_ref, ga_ref, gc_ref, o_ref, *, n_seq_tiles):
    i = pl.program_id(0)
    tm = u_ref.shape[0]
    o_ref[:, :ATTN_WIDTH] = (_rms(attn_ref[...].astype(F32)) * ga_ref[...]).astype(o_ref.dtype)

    u = u_ref[...].astype(F32)
    first = (i % n_seq_tiles) == 0
    last = (i % n_seq_tiles) == n_seq_tiles - 1
    prev_row = jnp.where(first, 0.0, up_ref[BF16_SUBLANES - 1:BF16_SUBLANES, :].astype(F32))
    next_row = jnp.where(last, 0.0, un_ref[0:1, :].astype(F32))
    row = lax.broadcasted_iota(jnp.int32, u.shape, 0)
    u_prev = jnp.where(row == 0, prev_row, pltpu.roll(u, 1, axis=0))
    u_next = jnp.where(row == tm - 1, next_row, pltpu.roll(u, tm - 1, axis=0))
    cw = cw_ref[...]
    conv = gb_ref[...].astype(F32) * (cw[0:1] * u_prev + cw[1:2] * u + cw[2:3] * u_next)
    o_ref[:, ATTN_WIDTH:] = (_rms(conv) * gc_ref[...]).astype(o_ref.dtype)


def premix(attn, gate_b, u, conv_w, g_attn, g_conv, seq_len, tm=256):
    t = attn.shape[0]
    tm = min(tm, seq_len)
    n_seq_tiles = seq_len // tm
    halo = tm // BF16_SUBLANES
    n_halo = t // BF16_SUBLANES
    row = lambda w: pl.BlockSpec((tm, w), lambda i: (i, 0))
    vec = lambda w: pl.BlockSpec((1, w), lambda i: (0, 0))
    return pl.pallas_call(
        functools.partial(_premix_kernel, n_seq_tiles=n_seq_tiles),
        grid=(t // tm,),
        in_specs=[row(ATTN_WIDTH), row(CONV_WIDTH), row(CONV_WIDTH),
                  pl.BlockSpec((BF16_SUBLANES, CONV_WIDTH), lambda i: (jnp.maximum(i * halo - 1, 0), 0)),
                  pl.BlockSpec((BF16_SUBLANES, CONV_WIDTH),
                               lambda i: (jnp.minimum((i + 1) * halo, n_halo - 1), 0)),
                  pl.BlockSpec((3, CONV_WIDTH), lambda i: (0, 0)),
                  vec(ATTN_WIDTH), vec(CONV_WIDTH)],
        out_specs=row(ATTN_WIDTH + CONV_WIDTH),
        out_shape=jax.ShapeDtypeStruct((t, ATTN_WIDTH + CONV_WIDTH), BF16),
        compiler_params=_cparams(("parallel",)),
        name="premix",
    )(attn, gate_b, u, u, u, conv_w, g_attn.reshape(1, -1), g_conv.reshape(1, -1))


def _rope_tables(seq_len):
    rows = seq_len // GRID_W
    inv = 1.0 / (ROPE_THETA ** (jnp.arange(0, ROPE_AXIS_DIM, 2, dtype=F32) / ROPE_AXIS_DIM))
    row_ang = jnp.arange(rows, dtype=F32)[:, None] * inv
    col_ang = jnp.arange(GRID_W, dtype=F32)[:, None] * inv
    ang = jnp.concatenate([
        jnp.broadcast_to(row_ang[:, None, :], (rows, GRID_W, inv.shape[0])),
        jnp.broadcast_to(col_ang[None, :, :], (rows, GRID_W, inv.shape[0])),
    ], axis=-1).reshape(seq_len, ROPE_AXIS_DIM)
    cos, sin = jnp.cos(ang), jnp.sin(ang)
    return jnp.concatenate([cos, cos], axis=-1), jnp.concatenate([-sin, sin], axis=-1)


def _prep_ffn(w_gate, w_up, w_down):
    return w_gate.astype(BF16), w_up.astype(BF16), w_down.astype(BF16)


def _ffn(hn, x, w, g_post, g_next):
    wg, wu, wd = w
    h = gate_up(hn, wg, wu, tm=4096, tn=FFN_TN)
    y = matmul(h, wd, tm=512, tn=512)
    return resid_norm_rows(y, x, g_post, FFN_RES_SCALE, g_next)


def _trunk(x3d, p):
    b, s, d = x3d.shape
    x = x3d.reshape(b * s, d)
    cos_full, sin_signed = _rope_tables(s)
    hn = rms_norm_rows(x, p["ffn1_pre"])
    x, hn = _ffn(hn, x, p["ffn1"], p["ffn1_post"], p["mix_pre"])
    qt, k, vt = inproj_qkv(hn, p["w_in"], p["q_norm"], p["k_norm"], cos_full, sin_signed, s)
    gate_b, u = inproj_conv(hn, p["w_in"])
    attn = attention(qt, k, vt, b, s)
    mix_in = premix(attn, gate_b, u, p["conv_w"], p["attn_out_norm"], p["conv_out_norm"], s)
    y = matmul(mix_in, p["w_out"])
    x, hn = resid_norm_rows(y, x, p["mix_post"], 1.0, p["ffn2_pre"])
    x, _ = _ffn(hn, x, p["ffn2"], p["ffn2_post"], None)
    return x.reshape(b, s, d)


def kernel(x_prompt, x_sample, ffn1_pre, ffn1_post, w1_gate, w1_up, w1_down, mix_pre, mix_post, w_in, q_norm, k_norm, conv_w, attn_out_norm, conv_out_norm, w_out, ffn2_pre, ffn2_post, w2_gate, w2_up, w2_down):
    p = {
        "ffn1_pre": ffn1_pre[0], "ffn1_post": ffn1_post[0],
        "ffn1": _prep_ffn(w1_gate[0], w1_up[0], w1_down[0]),
        "mix_pre": mix_pre[0], "mix_post": mix_post[0],
        "w_in": w_in[0].astype(BF16), "q_norm": q_norm[0], "k_norm": k_norm[0],
        "conv_w": conv_w[0], "attn_out_norm": attn_out_norm[0], "conv_out_norm": conv_out_norm[0],
        "w_out": w_out[0].astype(BF16),
        "ffn2_pre": ffn2_pre[0], "ffn2_post": ffn2_post[0],
        "ffn2": _prep_ffn(w2_gate[0], w2_up[0], w2_down[0]),
    }
    return _trunk(x_prompt, p), _trunk(x_sample, p)
```

```python
import functools

import jax
import jax.numpy as jnp
from jax import lax
from jax.experimental import pallas as pl
from jax.experimental.pallas import tpu as pltpu

F32 = jnp.float32
BF16 = jnp.bfloat16

NORM_EPS = 1e-6
FFN_RES_SCALE = 0.5
HEAD_DIM = 128
N_Q_HEADS = 16
N_KV_HEADS = 4
KV_GROUP = N_Q_HEADS // N_KV_HEADS
ATTN_WIDTH = N_Q_HEADS * HEAD_DIM
KV_WIDTH = N_KV_HEADS * HEAD_DIM
CONV_WIDTH = 2048
GRID_W = 64
ROPE_THETA = 10000.0
ROPE_AXIS_DIM = HEAD_DIM // 2

LANE = 128
BF16_SUBLANES = 16
FFN_TN = 256
VMEM_LIMIT_BYTES = 56 * 1024 * 1024


def _cparams(sem):
    return pltpu.CompilerParams(dimension_semantics=sem, vmem_limit_bytes=VMEM_LIMIT_BYTES)


def _rms(x):
    return x * lax.rsqrt(jnp.mean(x * x, axis=-1, keepdims=True) + NORM_EPS)


def _norm_kernel(x_ref, g_ref, o_ref):
    o_ref[...] = (_rms(x_ref[...]) * g_ref[...]).astype(o_ref.dtype)


def rms_norm_rows(x, g, tm=256):
    t, d = x.shape
    tm = min(tm, t)
    return pl.pallas_call(
        _norm_kernel,
        grid=(t // tm,),
        in_specs=[pl.BlockSpec((tm, d), lambda i: (i, 0)),
                  pl.BlockSpec((1, d), lambda i: (0, 0))],
        out_specs=pl.BlockSpec((tm, d), lambda i: (i, 0)),
        out_shape=jax.ShapeDtypeStruct((t, d), BF16),
        compiler_params=_cparams(("parallel",)),
        name="rms_norm_rows",
    )(x, g.reshape(1, d))


def _resid_norm_kernel(y_ref, x_ref, gp_ref, gn_ref, xo_ref, ho_ref, *, scale):
    x_new = x_ref[...] + scale * (_rms(y_ref[...]) * gp_ref[...])
    xo_ref[...] = x_new
    ho_ref[...] = (_rms(x_new) * gn_ref[...]).astype(ho_ref.dtype)


def _resid_kernel(y_ref, x_ref, gp_ref, xo_ref, *, scale):
    xo_ref[...] = x_ref[...] + scale * (_rms(y_ref[...]) * gp_ref[...])


def resid_norm_rows(y, x, g_post, scale, g_next=None, tm=256):
    t, d = x.shape
    tm = min(tm, t)
    row = pl.BlockSpec((tm, d), lambda i: (i, 0))
    vec = pl.BlockSpec((1, d), lambda i: (0, 0))
    if g_next is None:
        return pl.pallas_call(
            functools.partial(_resid_kernel, scale=scale),
            grid=(t // tm,),
            in_specs=[row, row, vec],
            out_specs=row,
            out_shape=jax.ShapeDtypeStruct((t, d), F32),
            compiler_params=_cparams(("parallel",)),
            name="resid_rows",
        )(y, x, g_post.reshape(1, d)), None
    return pl.pallas_call(
        functools.partial(_resid_norm_kernel, scale=scale),
        grid=(t // tm,),
        in_specs=[row, row, vec, vec],
        out_specs=[row, row],
        out_shape=[jax.ShapeDtypeStruct((t, d), F32), jax.ShapeDtypeStruct((t, d), BF16)],
        compiler_params=_cparams(("parallel",)),
        name="resid_norm_rows",
    )(y, x, g_post.reshape(1, d), g_next.reshape(1, d))


def _gateup_kernel(a_ref, wg_ref, wu_ref, o_ref):
    a = a_ref[...]
    g = jnp.dot(a, wg_ref[...], preferred_element_type=F32)
    u = jnp.dot(a, wu_ref[...], preferred_element_type=F32)
    o_ref[...] = (g * jax.nn.sigmoid(g) * u).astype(o_ref.dtype)


def gate_up(a, wg, wu, tm=1024, tn=512):
    t, d = a.shape
    f = wg.shape[1]
    tm, tn = min(tm, t), min(tn, f)
    return pl.pallas_call(
        _gateup_kernel,
        grid=(t // tm, f // tn),
        in_specs=[pl.BlockSpec((tm, d), lambda i, j: (i, 0), pipeline_mode=pl.Buffered(1)),
                  pl.BlockSpec((d, tn), lambda i, j: (0, j)),
                  pl.BlockSpec((d, tn), lambda i, j: (0, j))],
        out_specs=pl.BlockSpec((tm, tn), lambda i, j: (i, j)),
        out_shape=jax.ShapeDtypeStruct((t, f), BF16),
        compiler_params=_cparams(("parallel", "arbitrary")),
        name="gate_up",
    )(a, wg, wu)


def _mm_kernel(a_ref, w_ref, o_ref, acc_ref):
    k = pl.program_id(2)
    part = jnp.dot(a_ref[...], w_ref[...], preferred_element_type=F32)

    @pl.when(k == 0)
    def _():
        acc_ref[...] = part

    @pl.when(k > 0)
    def _():
        acc_ref[...] += part

    @pl.when(k == pl.num_programs(2) - 1)
    def _():
        o_ref[...] = acc_ref[...]


def _mm_fullk_kernel(a_ref, w_ref, o_ref):
    o_ref[...] = jnp.dot(a_ref[...], w_ref[...], preferred_element_type=F32)


def matmul(a, w, tm=1024, tn=1024, tk=None):
    t, kd = a.shape
    n = w.shape[1]
    tm, tn = min(tm, t), min(tn, n)
    if tk is None or tk >= kd:
        return pl.pallas_call(
            _mm_fullk_kernel,
            grid=(t // tm, n // tn),
            in_specs=[pl.BlockSpec((tm, kd), lambda i, j: (i, 0)),
                      pl.BlockSpec((kd, tn), lambda i, j: (0, j))],
            out_specs=pl.BlockSpec((tm, tn), lambda i, j: (i, j)),
            out_shape=jax.ShapeDtypeStruct((t, n), F32),
            compiler_params=_cparams(("parallel", "arbitrary")),
            name="matmul_fullk",
        )(a, w)
    return pl.pallas_call(
        _mm_kernel,
        grid=(t // tm, n // tn, kd // tk),
        in_specs=[pl.BlockSpec((tm, tk), lambda i, j, k: (i, k)),
                  pl.BlockSpec((tk, tn), lambda i, j, k: (k, j))],
        out_specs=pl.BlockSpec((tm, tn), lambda i, j, k: (i, j)),
        out_shape=jax.ShapeDtypeStruct((t, n), F32),
        scratch_shapes=[pltpu.VMEM((tm, tn), F32)],
        compiler_params=_cparams(("parallel", "arbitrary", "arbitrary")),
        name="matmul_kgrid",
    )(a, w)


QKV_TN = KV_GROUP * HEAD_DIM
N_Q_TILES = ATTN_WIDTH // QKV_TN
ATTN_TQ = 256
ATTN_TK = 1024
ATTN_SLAB = 64
ATTN_Q_TILES = 4
ATTN_CHUNK_UNROLL = 2
SCORE_BOUND_LOG2 = 40.0
Q_SCALE = (HEAD_DIM ** -0.5) * 1.4426950408889634


def _inproj_qkv_kernel(a_ref, w_ref, qg_ref, kg_ref, cos_ref, sin_ref, qt_ref, k_ref, vt_ref, raw_ref):
    j = pl.program_id(1)
    tm = a_ref.shape[0]
    n_steps = N_Q_TILES + 3

    def matmul(slot):
        raw_ref[slot] = jnp.dot(a_ref[...], w_ref[...], preferred_element_type=F32)

    def norm_rope(y, gain):
        y = _rms(y) * gain
        return y * cos_ref[...] + pltpu.roll(y, ROPE_AXIS_DIM, axis=1) * sin_ref[...]

    def store_transposed(dst_ref, h, y, width):
        yt = y.T.astype(dst_ref.dtype)
        for c in range(tm // width):
            dst_ref[h, c] = yt[:, c * width:(c + 1) * width]

    def epilogue(tile, slot):
        for h in range(KV_GROUP):
            y = raw_ref[slot, :, h * HEAD_DIM:(h + 1) * HEAD_DIM]
            if tile < N_Q_TILES:
                store_transposed(qt_ref, h, norm_rope(y, qg_ref[...]) * Q_SCALE, ATTN_TQ)
            elif tile == N_Q_TILES:
                k_ref[h] = norm_rope(y, kg_ref[...]).astype(k_ref.dtype)
            else:
                store_transposed(vt_ref, h, y, ATTN_TK)

    for step in range(n_steps):
        @pl.when(j == step)
        def _(step=step):
            if step < n_steps - 1:
                matmul(step % 2)
            if step > 0:
                epilogue(step - 1, (step - 1) % 2)


def inproj_qkv(a, w_in, q_norm, k_norm, cos_full, sin_signed, seq_len, tm=1024):
    t, d = a.shape
    tm = min(tm, seq_len)
    n_seq_tiles = seq_len // tm
    return pl.pallas_call(
        _inproj_qkv_kernel,
        grid=(t // tm, N_Q_TILES + 3),
        in_specs=[pl.BlockSpec((tm, d), lambda i, j: (i, 0)),
                  pl.BlockSpec((d, QKV_TN), lambda i, j: (0, jnp.minimum(j, N_Q_TILES + 1))),
                  pl.BlockSpec((1, HEAD_DIM), lambda i, j: (0, 0)),
                  pl.BlockSpec((1, HEAD_DIM), lambda i, j: (0, 0)),
                  pl.BlockSpec((tm, HEAD_DIM), lambda i, j: (i % n_seq_tiles, 0)),
                  pl.BlockSpec((tm, HEAD_DIM), lambda i, j: (i % n_seq_tiles, 0))],
        out_specs=[pl.BlockSpec((KV_GROUP, tm // ATTN_TQ, HEAD_DIM, ATTN_TQ),
                                lambda i, j: (jnp.clip(j - 1, 0, N_Q_TILES - 1), i, 0, 0)),
                   pl.BlockSpec((N_KV_HEADS, tm, HEAD_DIM), lambda i, j: (0, i, 0)),
                   pl.BlockSpec((N_KV_HEADS, tm // ATTN_TK, HEAD_DIM, ATTN_TK), lambda i, j: (0, i, 0, 0))],
        out_shape=[jax.ShapeDtypeStruct((N_Q_HEADS, t // ATTN_TQ, HEAD_DIM, ATTN_TQ), BF16),
                   jax.ShapeDtypeStruct((N_KV_HEADS, t, HEAD_DIM), BF16),
                   jax.ShapeDtypeStruct((N_KV_HEADS, t // ATTN_TK, HEAD_DIM, ATTN_TK), BF16)],
        scratch_shapes=[pltpu.VMEM((2, tm, QKV_TN), F32)],
        compiler_params=_cparams(("parallel", "arbitrary")),
        name="inproj_qkv",
    )(a, w_in, q_norm.reshape(1, HEAD_DIM), k_norm.reshape(1, HEAD_DIM), cos_full, sin_signed)


def _inproj_conv_kernel(a_ref, wb_ref, wc_ref, wh_ref, gb_ref, u_ref):
    a = a_ref[...]
    gb_ref[...] = jnp.dot(a, wb_ref[...], preferred_element_type=F32).astype(gb_ref.dtype)
    c = jnp.dot(a, wc_ref[...], preferred_element_type=F32)
    h = jnp.dot(a, wh_ref[...], preferred_element_type=F32)
    u_ref[...] = (c * h).astype(u_ref.dtype)


def inproj_conv(a, w_in, tm=1024, tn=512):
    t, d = a.shape
    tm = min(tm, t)
    off_b = (ATTN_WIDTH + 2 * KV_WIDTH) // tn
    off_c = off_b + CONV_WIDTH // tn
    off_h = off_c + CONV_WIDTH // tn
    out = jax.ShapeDtypeStruct((t, CONV_WIDTH), BF16)
    return pl.pallas_call(
        _inproj_conv_kernel,
        grid=(t // tm, CONV_WIDTH // tn),
        in_specs=[pl.BlockSpec((tm, d), lambda i, j: (i, 0)),
                  pl.BlockSpec((d, tn), lambda i, j: (0, off_b + j)),
                  pl.BlockSpec((d, tn), lambda i, j: (0, off_c + j)),
                  pl.BlockSpec((d, tn), lambda i, j: (0, off_h + j))],
        out_specs=[pl.BlockSpec((tm, tn), lambda i, j: (i, j)),
                   pl.BlockSpec((tm, tn), lambda i, j: (i, j))],
        out_shape=[out, out],
        compiler_params=_cparams(("parallel", "arbitrary")),
        name="inproj_conv",
    )(a, w_in, w_in, w_in)


def _attn_kernel(qt_ref, k_ref, vt_ref, o_ref, ksq_ref, m_ref, l_ref, acc_ref, s_ref, p_ref, alpha_ref):
    group, n_q_tiles, _, tq = qt_ref.shape
    n_heads = group * n_q_tiles
    n_chunks = vt_ref.shape[1]
    tk = vt_ref.shape[3]
    slabs = [pl.ds(r, ATTN_SLAB) for r in range(0, tk, ATTN_SLAB)]

    def q_tile(h):
        return qt_ref[h % group, h // group]

    def key_chunk(c):
        return k_ref[0, pl.ds(pl.multiple_of(c * tk, tk), tk), :]

    @pl.when(pl.program_id(2) == 0)
    def _():
        def chunk_max(c, best):
            kc = key_chunk(c).astype(F32)
            return jnp.maximum(best, jnp.max(jnp.sum(kc * kc, axis=1, keepdims=True)))
        ksq_ref[0] = lax.fori_loop(0, n_chunks, chunk_max, jnp.float32(0.0))

    qsq = jnp.float32(0.0)
    for h in range(n_heads):
        q = q_tile(h).astype(F32)
        qsq = jnp.maximum(qsq, jnp.max(jnp.sum(q * q, axis=0, keepdims=True)))
    scores_bounded = qsq * ksq_ref[0] <= SCORE_BOUND_LOG2 * SCORE_BOUND_LOG2

    l_ref[...] = jnp.zeros(l_ref.shape, F32)
    acc_ref[...] = jnp.zeros(acc_ref.shape, F32)
    p_ref[...] = jnp.zeros(p_ref.shape, p_ref.dtype)

    def pipeline(stage_scores, stage_values, unroll):
        unroll = unroll if n_chunks % unroll == 0 else 1

        @pl.loop(0, n_chunks // unroll)
        def _(step):
            for sub in range(unroll):
                c = step * unroll + sub
                for h in range(n_heads):
                    slot, other = h % 2, (h + 1) % 2
                    stage_scores(c, h, slot)
                    if h > 0:
                        stage_values(c, h - 1, other)
                    else:
                        stage_values(jnp.maximum(c - 1, 0), n_heads - 1, other)
        stage_values(n_chunks - 1, n_heads - 1, (n_heads - 1) % 2)

    def unshifted():
        def probs(c, h, slot):
            s = jnp.dot(key_chunk(c), q_tile(h), preferred_element_type=F32)
            slab_sum = jnp.zeros((ATTN_SLAB, s.shape[1]), F32)
            for r in range(0, tk, ATTN_SLAB):
                p = jnp.exp2(s[r:r + ATTN_SLAB])
                slab_sum = slab_sum + p
                p_ref[slot, pl.ds(r, ATTN_SLAB), :] = p.astype(p_ref.dtype)
            l_ref[h] = l_ref[h] + jnp.sum(slab_sum, axis=0, keepdims=True)

        def values(c, h, slot):
            acc_ref[h] = acc_ref[h] + jnp.dot(vt_ref[0, c], p_ref[slot], preferred_element_type=F32)

        pipeline(probs, values, ATTN_CHUNK_UNROLL)

    def running_max():
        def scores(c, h, slot):
            s_ref[slot] = jnp.dot(key_chunk(c), q_tile(h), preferred_element_type=F32)

        def softmax(h, slot):
            m_old = m_ref[h]
            slab_max = s_ref[slot, slabs[0], :]
            for rows in slabs[1:]:
                slab_max = jnp.maximum(slab_max, s_ref[slot, rows, :])
            m_new = jnp.maximum(m_old, jnp.max(slab_max, axis=0, keepdims=True))
            alpha = jnp.exp2(m_old - m_new)
            slab_sum = jnp.zeros(slab_max.shape, F32)
            for rows in slabs:
                p = jnp.exp2(s_ref[slot, rows, :] - m_new)
                slab_sum = slab_sum + p
                p_ref[slot, rows, :] = p.astype(p_ref.dtype)
            l_ref[h] = alpha * l_ref[h] + jnp.sum(slab_sum, axis=0, keepdims=True)
            m_ref[h] = m_new
            alpha_ref[slot] = alpha

        def scores_and_softmax(c, h, slot):
            if h + 1 < n_heads:
                scores(c, h + 1, (h + 1) % 2)
            else:
                scores(jnp.minimum(c + 1, n_chunks - 1), 0, (h + 1) % 2)
            softmax(h, slot)

        def values(c, h, slot):
            pv = jnp.dot(vt_ref[0, c], p_ref[slot], preferred_element_type=F32)
            acc_ref[h] = alpha_ref[slot] * acc_ref[h] + pv

        m_ref[...] = jnp.full(m_ref.shape, -jnp.inf, F32)
        alpha_ref[...] = jnp.ones(alpha_ref.shape, F32)
        scores(0, 0, 0)
        pipeline(scores_and_softmax, values, 1)

    lax.cond(scores_bounded, unshifted, running_max)

    for h in range(n_heads):
        out_t = acc_ref[h] / l_ref[h]
        head, tile = h % group, h // group
        o_ref[tile * tq:(tile + 1) * tq, head * HEAD_DIM:(head + 1) * HEAD_DIM] = out_t.T.astype(o_ref.dtype)


def attention(qt, k, vt, batch, seq_len):
    t = k.shape[1]
    q_tiles = min(ATTN_Q_TILES, seq_len // ATTN_TQ)
    nq = seq_len // (ATTN_TQ * q_tiles)
    nk = seq_len // ATTN_TK
    n_units = KV_GROUP * q_tiles
    return pl.pallas_call(
        _attn_kernel,
        grid=(batch, N_KV_HEADS, nq),
        in_specs=[pl.BlockSpec((KV_GROUP, q_tiles, HEAD_DIM, ATTN_TQ), lambda b, g, i: (g, b * nq + i, 0, 0)),
                  pl.BlockSpec((1, seq_len, HEAD_DIM), lambda b, g, i: (g, b, 0)),
                  pl.BlockSpec((1, nk, HEAD_DIM, ATTN_TK), lambda b, g, i: (g, b, 0, 0))],
        out_specs=pl.BlockSpec((ATTN_TQ * q_tiles, KV_GROUP * HEAD_DIM), lambda b, g, i: (b * nq + i, g)),
        out_shape=jax.ShapeDtypeStruct((t, ATTN_WIDTH), BF16),
        scratch_shapes=[pltpu.SMEM((1,), F32),
                        pltpu.VMEM((n_units, 1, ATTN_TQ), F32),
                        pltpu.VMEM((n_units, 1, ATTN_TQ), F32),
                        pltpu.VMEM((n_units, HEAD_DIM, ATTN_TQ), F32),
                        pltpu.VMEM((2, ATTN_TK, ATTN_TQ), F32),
                        pltpu.VMEM((2, ATTN_TK, ATTN_TQ), BF16),
                        pltpu.VMEM((2, 1, ATTN_TQ), F32)],
        compiler_params=_cparams(("parallel", "parallel", "arbitrary")),
        name="attention",
    )(qt, k, vt)


def _premix_kernel(attn_ref, gb_ref, u_ref, up_ref, un_ref, cw_ref, ga_ref, gc_ref, o_ref, *, n_seq_tiles):
    i = pl.program_id(0)
    tm = u_ref.shape[0]
    o_ref[:, :ATTN_WIDTH] = (_rms(attn_ref[...].astype(F32)) * ga_ref[...]).astype(o_ref.dtype)

    u = u_ref[...].astype(F32)
    first = (i % n_seq_tiles) == 0
    last = (i % n_seq_tiles) == n_seq_tiles - 1
    prev_row = jnp.where(first, 0.0, up_ref[BF16_SUBLANES - 1:BF16_SUBLANES, :].astype(F32))
    next_row = jnp.where(last, 0.0, un_ref[0:1, :].astype(F32))
    row = lax.broadcasted_iota(jnp.int32, u.shape, 0)
    u_prev = jnp.where(row == 0, prev_row, pltpu.roll(u, 1, axis=0))
    u_next = jnp.where(row == tm - 1, next_row, pltpu.roll(u, tm - 1, axis=0))
    cw = cw_ref[...]
    conv = gb_ref[...].astype(F32) * (cw[0:1] * u_prev + cw[1:2] * u + cw[2:3] * u_next)
    o_ref[:, ATTN_WIDTH:] = (_rms(conv) * gc_ref[...]).astype(o_ref.dtype)


def premix(attn, gate_b, u, conv_w, g_attn, g_conv, seq_len, tm=256):
    t = attn.shape[0]
    tm = min(tm, seq_len)
    n_seq_tiles = seq_len // tm
    halo = tm // BF16_SUBLANES
    n_halo = t // BF16_SUBLANES
    row = lambda w: pl.BlockSpec((tm, w), lambda i: (i, 0))
    vec = lambda w: pl.BlockSpec((1, w), lambda i: (0, 0))
    return pl.pallas_call(
        functools.partial(_premix_kernel, n_seq_tiles=n_seq_tiles),
        grid=(t // tm,),
        in_specs=[row(ATTN_WIDTH), row(CONV_WIDTH), row(CONV_WIDTH),
                  pl.BlockSpec((BF16_SUBLANES, CONV_WIDTH), lambda i: (jnp.maximum(i * halo - 1, 0), 0)),
                  pl.BlockSpec((BF16_SUBLANES, CONV_WIDTH),
                               lambda i: (jnp.minimum((i + 1) * halo, n_halo - 1), 0)),
                  pl.BlockSpec((3, CONV_WIDTH), lambda i: (0, 0)),
                  vec(ATTN_WIDTH), vec(CONV_WIDTH)],
        out_specs=row(ATTN_WIDTH + CONV_WIDTH),
        out_shape=jax.ShapeDtypeStruct((t, ATTN_WIDTH + CONV_WIDTH), BF16),
        compiler_params=_cparams(("parallel",)),
        name="premix",
    )(attn, gate_b, u, u, u, conv_w, g_attn.reshape(1, -1), g_conv.reshape(1, -1))


def _rope_tables(seq_len):
    rows = seq_len // GRID_W
    inv = 1.0 / (ROPE_THETA ** (jnp.arange(0, ROPE_AXIS_DIM, 2, dtype=F32) / ROPE_AXIS_DIM))
    row_ang = jnp.arange(rows, dtype=F32)[:, None] * inv
    col_ang = jnp.arange(GRID_W, dtype=F32)[:, None] * inv
    ang = jnp.concatenate([
        jnp.broadcast_to(row_ang[:, None, :], (rows, GRID_W, inv.shape[0])),
        jnp.broadcast_to(col_ang[None, :, :], (rows, GRID_W, inv.shape[0])),
    ], axis=-1).reshape(seq_len, ROPE_AXIS_DIM)
    cos, sin = jnp.cos(ang), jnp.sin(ang)
    return jnp.concatenate([cos, cos], axis=-1), jnp.concatenate([-sin, sin], axis=-1)


def _prep_ffn(w_gate, w_up, w_down):
    return w_gate.astype(BF16), w_up.astype(BF16), w_down.astype(BF16)


def _ffn(hn, x, w, g_post, g_next):
    wg, wu, wd = w
    h = gate_up(hn, wg, wu, tm=4096, tn=FFN_TN)
    y = matmul(h, wd, tm=512, tn=512)
    return resid_norm_rows(y, x, g_post, FFN_RES_SCALE, g_next)


def _trunk(x3d, p):
    b, s, d = x3d.shape
    x = x3d.reshape(b * s, d)
    cos_full, sin_signed = _rope_tables(s)
    hn = rms_norm_rows(x, p["ffn1_pre"])
    x, hn = _ffn(hn, x, p["ffn1"], p["ffn1_post"], p["mix_pre"])
    qt, k, vt = inproj_qkv(hn, p["w_in"], p["q_norm"], p["k_norm"], cos_full, sin_signed, s)
    gate_b, u = inproj_conv(hn, p["w_in"])
    attn = attention(qt, k, vt, b, s)
    mix_in = premix(attn, gate_b, u, p["conv_w"], p["attn_out_norm"], p["conv_out_norm"], s)
    y = matmul(mix_in, p["w_out"])
    x, hn = resid_norm_rows(y, x, p["mix_post"], 1.0, p["ffn2_pre"])
    x, _ = _ffn(hn, x, p["ffn2"], p["ffn2_post"], None)
    return x.reshape(b, s, d)


def kernel(x_prompt, x_sample, ffn1_pre, ffn1_post, w1_gate, w1_up, w1_down, mix_pre, mix_post, w_in, q_norm, k_norm, conv_w, attn_out_norm, conv_out_norm, w_out, ffn2_pre, ffn2_post, w2_gate, w2_up, w2_down):
    p = {
        "ffn1_pre": ffn1_pre[0], "ffn1_post": ffn1_post[0],
        "ffn1": _prep_ffn(w1_gate[0], w1_up[0], w1_down[0]),
        "mix_pre": mix_pre[0], "mix_post": mix_post[0],
        "w_in": w_in[0].astype(BF16), "q_norm": q_norm[0], "k_norm": k_norm[0],
        "conv_w": conv_w[0], "attn_out_norm": attn_out_norm[0], "conv_out_norm": conv_out_norm[0],
        "w_out": w_out[0].astype(BF16),
        "ffn2_pre": ffn2_pre[0], "ffn2_post": ffn2_post[0],
        "ffn2": _prep_ffn(w2_gate[0], w2_up[0], w2_down[0]),
    }
    return _trunk(x_prompt, p), _trunk(x_sample, p)
```

```python
import functools

import jax
import jax.numpy as jnp
from jax import lax
from jax.experimental import pallas as pl
from jax.experimental.pallas import tpu as pltpu

F32 = jnp.float32
BF16 = jnp.bfloat16

NORM_EPS = 1e-6
FFN_RES_SCALE = 0.5
HEAD_DIM = 128
N_Q_HEADS = 16
N_KV_HEADS = 4
KV_GROUP = N_Q_HEADS // N_KV_HEADS
ATTN_WIDTH = N_Q_HEADS * HEAD_DIM
KV_WIDTH = N_KV_HEADS * HEAD_DIM
CONV_WIDTH = 2048
GRID_W = 64
ROPE_THETA = 10000.0
ROPE_AXIS_DIM = HEAD_DIM // 2

BF16_SUBLANES = 16
VMEM_LIMIT_BYTES = 56 * 1024 * 1024
GATE_UP_TILE = (4096, 256)
DOWN_TILE = (512, 512)
OUT_PROJ_TILE = (1024, 1024)
IN_PROJ_TM = 1024
ROW_TILE = 256


def _cparams(sem):
    return pltpu.CompilerParams(dimension_semantics=sem, vmem_limit_bytes=VMEM_LIMIT_BYTES)


def _rms(x):
    return x * lax.rsqrt(jnp.mean(x * x, axis=-1, keepdims=True) + NORM_EPS)


def _norm_kernel(x_ref, g_ref, o_ref):
    o_ref[...] = (_rms(x_ref[...]) * g_ref[...]).astype(o_ref.dtype)


def rms_norm_rows(x, g, tm=ROW_TILE):
    t, d = x.shape
    tm = min(tm, t)
    return pl.pallas_call(
        _norm_kernel,
        grid=(t // tm,),
        in_specs=[pl.BlockSpec((tm, d), lambda i: (i, 0)),
                  pl.BlockSpec((1, d), lambda i: (0, 0))],
        out_specs=pl.BlockSpec((tm, d), lambda i: (i, 0)),
        out_shape=jax.ShapeDtypeStruct((t, d), BF16),
        compiler_params=_cparams(("parallel",)),
        name="rms_norm_rows",
    )(x, g.reshape(1, d))


def _resid_norm_kernel(y_ref, x_ref, gp_ref, gn_ref, xo_ref, ho_ref, *, scale):
    x_new = x_ref[...] + scale * (_rms(y_ref[...]) * gp_ref[...])
    xo_ref[...] = x_new
    ho_ref[...] = (_rms(x_new) * gn_ref[...]).astype(ho_ref.dtype)


def _resid_kernel(y_ref, x_ref, gp_ref, xo_ref, *, scale):
    xo_ref[...] = x_ref[...] + scale * (_rms(y_ref[...]) * gp_ref[...])


def resid_norm_rows(y, x, g_post, scale, g_next=None, tm=ROW_TILE):
    t, d = x.shape
    tm = min(tm, t)
    row = pl.BlockSpec((tm, d), lambda i: (i, 0))
    vec = pl.BlockSpec((1, d), lambda i: (0, 0))
    if g_next is None:
        return pl.pallas_call(
            functools.partial(_resid_kernel, scale=scale),
            grid=(t // tm,),
            in_specs=[row, row, vec],
            out_specs=row,
            out_shape=jax.ShapeDtypeStruct((t, d), F32),
            compiler_params=_cparams(("parallel",)),
            name="resid_rows",
        )(y, x, g_post.reshape(1, d)), None
    return pl.pallas_call(
        functools.partial(_resid_norm_kernel, scale=scale),
        grid=(t // tm,),
        in_specs=[row, row, vec, vec],
        out_specs=[row, row],
        out_shape=[jax.ShapeDtypeStruct((t, d), F32), jax.ShapeDtypeStruct((t, d), BF16)],
        compiler_params=_cparams(("parallel",)),
        name="resid_norm_rows",
    )(y, x, g_post.reshape(1, d), g_next.reshape(1, d))


def _gateup_kernel(a_ref, wg_ref, wu_ref, o_ref):
    a = a_ref[...]
    g = jnp.dot(a, wg_ref[...], preferred_element_type=F32)
    u = jnp.dot(a, wu_ref[...], preferred_element_type=F32)
    o_ref[...] = (g * jax.nn.sigmoid(g) * u).astype(o_ref.dtype)


def gate_up(a, wg, wu):
    t, d = a.shape
    f = wg.shape[1]
    tm, tn = min(GATE_UP_TILE[0], t), GATE_UP_TILE[1]
    return pl.pallas_call(
        _gateup_kernel,
        grid=(t // tm, f // tn),
        in_specs=[pl.BlockSpec((tm, d), lambda i, j: (i, 0), pipeline_mode=pl.Buffered(1)),
                  pl.BlockSpec((d, tn), lambda i, j: (0, j)),
                  pl.BlockSpec((d, tn), lambda i, j: (0, j))],
        out_specs=pl.BlockSpec((tm, tn), lambda i, j: (i, j)),
        out_shape=jax.ShapeDtypeStruct((t, f), BF16),
        compiler_params=_cparams(("parallel", "arbitrary")),
        name="gate_up",
    )(a, wg, wu)


def _mm_kernel(a_ref, w_ref, o_ref):
    o_ref[...] = jnp.dot(a_ref[...], w_ref[...], preferred_element_type=F32)


def matmul(a, w, tm, tn):
    t, kd = a.shape
    n = w.shape[1]
    tm, tn = min(tm, t), min(tn, n)
    return pl.pallas_call(
        _mm_kernel,
        grid=(t // tm, n // tn),
        in_specs=[pl.BlockSpec((tm, kd), lambda i, j: (i, 0)),
                  pl.BlockSpec((kd, tn), lambda i, j: (0, j))],
        out_specs=pl.BlockSpec((tm, tn), lambda i, j: (i, j)),
        out_shape=jax.ShapeDtypeStruct((t, n), F32),
        compiler_params=_cparams(("parallel", "arbitrary")),
        name="matmul_fullk",
    )(a, w)


QKV_TN = KV_GROUP * HEAD_DIM
N_Q_TILES = ATTN_WIDTH // QKV_TN
ATTN_TQ = 256
ATTN_TK = 1024
ATTN_SLAB = 64
ATTN_Q_TILES = 4
ATTN_CHUNK_UNROLL = 4
SCORE_BOUND_LOG2 = 40.0
Q_SCALE = (HEAD_DIM ** -0.5) * 1.4426950408889634


def _inproj_qkv_kernel(a_ref, w_ref, qg_ref, kg_ref, cos_ref, sin_ref, qt_ref, k_ref, vt_ref, raw_ref):
    j = pl.program_id(1)
    tm = a_ref.shape[0]
    n_steps = N_Q_TILES + 3

    def matmul(slot):
        raw_ref[slot] = jnp.dot(a_ref[...], w_ref[...], preferred_element_type=F32)

    def norm_rope(y, gain):
        y = _rms(y) * gain
        return y * cos_ref[...] + pltpu.roll(y, ROPE_AXIS_DIM, axis=1) * sin_ref[...]

    def store_transposed(dst_ref, h, y, width):
        yt = y.T.astype(dst_ref.dtype)
        for c in range(tm // width):
            dst_ref[h, c] = yt[:, c * width:(c + 1) * width]

    def epilogue(tile, slot):
        for h in range(KV_GROUP):
            y = raw_ref[slot, :, h * HEAD_DIM:(h + 1) * HEAD_DIM]
            if tile < N_Q_TILES:
                store_transposed(qt_ref, h, norm_rope(y, qg_ref[...]) * Q_SCALE, ATTN_TQ)
            elif tile == N_Q_TILES:
                k_ref[h] = norm_rope(y, kg_ref[...]).astype(k_ref.dtype)
            else:
                store_transposed(vt_ref, h, y, ATTN_TK)

    for step in range(n_steps):
        @pl.when(j == step)
        def _(step=step):
            if step < n_steps - 1:
                matmul(step % 2)
            if step > 0:
                epilogue(step - 1, (step - 1) % 2)


def inproj_qkv(a, w_in, q_norm, k_norm, cos_full, sin_signed, seq_len, tm=IN_PROJ_TM):
    t, d = a.shape
    tm = min(tm, seq_len)
    n_seq_tiles = seq_len // tm
    return pl.pallas_call(
        _inproj_qkv_kernel,
        grid=(t // tm, N_Q_TILES + 3),
        in_specs=[pl.BlockSpec((tm, d), lambda i, j: (i, 0)),
                  pl.BlockSpec((d, QKV_TN), lambda i, j: (0, jnp.minimum(j, N_Q_TILES + 1))),
                  pl.BlockSpec((1, HEAD_DIM), lambda i, j: (0, 0)),
                  pl.BlockSpec((1, HEAD_DIM), lambda i, j: (0, 0)),
                  pl.BlockSpec((tm, HEAD_DIM), lambda i, j: (i % n_seq_tiles, 0)),
                  pl.BlockSpec((tm, HEAD_DIM), lambda i, j: (i % n_seq_tiles, 0))],
        out_specs=[pl.BlockSpec((KV_GROUP, tm // ATTN_TQ, HEAD_DIM, ATTN_TQ),
                                lambda i, j: (jnp.clip(j - 1, 0, N_Q_TILES - 1), i, 0, 0)),
                   pl.BlockSpec((N_KV_HEADS, tm, HEAD_DIM), lambda i, j: (0, i, 0)),
                   pl.BlockSpec((N_KV_HEADS, tm // ATTN_TK, HEAD_DIM, ATTN_TK), lambda i, j: (0, i, 0, 0))],
        out_shape=[jax.ShapeDtypeStruct((N_Q_HEADS, t // ATTN_TQ, HEAD_DIM, ATTN_TQ), BF16),
                   jax.ShapeDtypeStruct((N_KV_HEADS, t, HEAD_DIM), BF16),
                   jax.ShapeDtypeStruct((N_KV_HEADS, t // ATTN_TK, HEAD_DIM, ATTN_TK), BF16)],
        scratch_shapes=[pltpu.VMEM((2, tm, QKV_TN), F32)],
        compiler_params=_cparams(("parallel", "arbitrary")),
        name="inproj_qkv",
    )(a, w_in, q_norm.reshape(1, HEAD_DIM), k_norm.reshape(1, HEAD_DIM), cos_full, sin_signed)


def _inproj_conv_kernel(a_ref, wb_ref, wc_ref, wh_ref, gb_ref, u_ref):
    a = a_ref[...]
    gb_ref[...] = jnp.dot(a, wb_ref[...], preferred_element_type=F32).astype(gb_ref.dtype)
    c = jnp.dot(a, wc_ref[...], preferred_element_type=F32)
    h = jnp.dot(a, wh_ref[...], preferred_element_type=F32)
    u_ref[...] = (c * h).astype(u_ref.dtype)


def inproj_conv(a, w_in, tm=IN_PROJ_TM, tn=QKV_TN):
    t, d = a.shape
    tm = min(tm, t)
    off_b = (ATTN_WIDTH + 2 * KV_WIDTH) // tn
    off_c = off_b + CONV_WIDTH // tn
    off_h = off_c + CONV_WIDTH // tn
    out = jax.ShapeDtypeStruct((t, CONV_WIDTH), BF16)
    return pl.pallas_call(
        _inproj_conv_kernel,
        grid=(t // tm, CONV_WIDTH // tn),
        in_specs=[pl.BlockSpec((tm, d), lambda i, j: (i, 0)),
                  pl.BlockSpec((d, tn), lambda i, j: (0, off_b + j)),
                  pl.BlockSpec((d, tn), lambda i, j: (0, off_c + j)),
                  pl.BlockSpec((d, tn), lambda i, j: (0, off_h + j))],
        out_specs=[pl.BlockSpec((tm, tn), lambda i, j: (i, j)),
                   pl.BlockSpec((tm, tn), lambda i, j: (i, j))],
        out_shape=[out, out],
        compiler_params=_cparams(("parallel", "arbitrary")),
        name="inproj_conv",
    )(a, w_in, w_in, w_in)


def _attn_kernel(qt_ref, k_ref, vt_ref, o_ref, ksq_ref, m_ref, l_ref, acc_ref, s_ref, p_ref, alpha_ref):
    group, n_q_tiles, _, tq = qt_ref.shape
    n_heads = group * n_q_tiles
    n_chunks = vt_ref.shape[1]
    tk = vt_ref.shape[3]
    slabs = [pl.ds(r, ATTN_SLAB) for r in range(0, tk, ATTN_SLAB)]

    def q_tile(h):
        return qt_ref[h % group, h // group]

    def key_chunk(c):
        return k_ref[0, pl.ds(pl.multiple_of(c * tk, tk), tk), :]

    @pl.when(pl.program_id(2) == 0)
    def _():
        def chunk_max(c, best):
            kc = key_chunk(c).astype(F32)
            return jnp.maximum(best, jnp.max(jnp.sum(kc * kc, axis=1, keepdims=True)))
        ksq_ref[0] = lax.fori_loop(0, n_chunks, chunk_max, jnp.float32(0.0))

    qsq = jnp.float32(0.0)
    for h in range(n_heads):
        q = q_tile(h).astype(F32)
        qsq = jnp.maximum(qsq, jnp.max(jnp.sum(q * q, axis=0, keepdims=True)))
    scores_bounded = qsq * ksq_ref[0] <= SCORE_BOUND_LOG2 * SCORE_BOUND_LOG2

    l_ref[...] = jnp.zeros(l_ref.shape, F32)
    acc_ref[...] = jnp.zeros(acc_ref.shape, F32)
    p_ref[...] = jnp.zeros(p_ref.shape, p_ref.dtype)

    def pipeline(stage_scores, stage_values, unroll):
        unroll = unroll if n_chunks % unroll == 0 else 1

        @pl.loop(0, n_chunks // unroll)
        def _(step):
            for sub in range(unroll):
                c = step * unroll + sub
                for h in range(n_heads):
                    slot, other = h % 2, (h + 1) % 2
                    stage_scores(c, h, slot)
                    if h > 0:
                        stage_values(c, h - 1, other)
                    else:
                        stage_values(jnp.maximum(c - 1, 0), n_heads - 1, other)
        stage_values(n_chunks - 1, n_heads - 1, (n_heads - 1) % 2)

    def unshifted():
        def probs(c, h, slot):
            s = jnp.dot(key_chunk(c), q_tile(h), preferred_element_type=F32)
            slab_sum = jnp.zeros((ATTN_SLAB, s.shape[1]), F32)
            for r in range(0, tk, ATTN_SLAB):
                p = jnp.exp2(s[r:r + ATTN_SLAB])
                slab_sum = slab_sum + p
                p_ref[slot, pl.ds(r, ATTN_SLAB), :] = p.astype(p_ref.dtype)
            l_ref[h] = l_ref[h] + jnp.sum(slab_sum, axis=0, keepdims=True)

        def values(c, h, slot):
            acc_ref[h] = acc_ref[h] + jnp.dot(vt_ref[0, c], p_ref[slot], preferred_element_type=F32)

        pipeline(probs, values, ATTN_CHUNK_UNROLL)

    def running_max():
        def scores(c, h, slot):
            s_ref[slot] = jnp.dot(key_chunk(c), q_tile(h), preferred_element_type=F32)

        def softmax(h, slot):
            m_old = m_ref[h]
            slab_max = s_ref[slot, slabs[0], :]
            for rows in slabs[1:]:
                slab_max = jnp.maximum(slab_max, s_ref[slot, rows, :])
            m_new = jnp.maximum(m_old, jnp.max(slab_max, axis=0, keepdims=True))
            alpha = jnp.exp2(m_old - m_new)
            slab_sum = jnp.zeros(slab_max.shape, F32)
            for rows in slabs:
                p = jnp.exp2(s_ref[slot, rows, :] - m_new)
                slab_sum = slab_sum + p
                p_ref[slot, rows, :] = p.astype(p_ref.dtype)
            l_ref[h] = alpha * l_ref[h] + jnp.sum(slab_sum, axis=0, keepdims=True)
            m_ref[h] = m_new
            alpha_ref[slot] = alpha

        def scores_and_softmax(c, h, slot):
            if h + 1 < n_heads:
                scores(c, h + 1, (h + 1) % 2)
            else:
                scores(jnp.minimum(c + 1, n_chunks - 1), 0, (h + 1) % 2)
            softmax(h, slot)

        def values(c, h, slot):
            pv = jnp.dot(vt_ref[0, c], p_ref[slot], preferred_element_type=F32)
            acc_ref[h] = alpha_ref[slot] * acc_ref[h] + pv

        m_ref[...] = jnp.full(m_ref.shape, -jnp.inf, F32)
        alpha_ref[...] = jnp.ones(alpha_ref.shape, F32)
        scores(0, 0, 0)
        pipeline(scores_and_softmax, values, 1)

    lax.cond(scores_bounded, unshifted, running_max)

    for h in range(n_heads):
        out_t = acc_ref[h] / l_ref[h]
        head, tile = h % group, h // group
        o_ref[tile * tq:(tile + 1) * tq, head * HEAD_DIM:(head + 1) * HEAD_DIM] = out_t.T.astype(o_ref.dtype)


def attention(qt, k, vt, batch, seq_len):
    t = k.shape[1]
    q_tiles = min(ATTN_Q_TILES, seq_len // ATTN_TQ)
    nq = seq_len // (ATTN_TQ * q_tiles)
    nk = seq_len // ATTN_TK
    n_units = KV_GROUP * q_tiles
    return pl.pallas_call(
        _attn_kernel,
        grid=(batch, N_KV_HEADS, nq),
        in_specs=[pl.BlockSpec((KV_GROUP, q_tiles, HEAD_DIM, ATTN_TQ), lambda b, g, i: (g, b * nq + i, 0, 0)),
                  pl.BlockSpec((1, seq_len, HEAD_DIM), lambda b, g, i: (g, b, 0)),
                  pl.BlockSpec((1, nk, HEAD_DIM, ATTN_TK), lambda b, g, i: (g, b, 0, 0))],
        out_specs=pl.BlockSpec((ATTN_TQ * q_tiles, KV_GROUP * HEAD_DIM), lambda b, g, i: (b * nq + i, g)),
        out_shape=jax.ShapeDtypeStruct((t, ATTN_WIDTH), BF16),
        scratch_shapes=[pltpu.SMEM((1,), F32),
                        pltpu.VMEM((n_units, 1, ATTN_TQ), F32),
                        pltpu.VMEM((n_units, 1, ATTN_TQ), F32),
                        pltpu.VMEM((n_units, HEAD_DIM, ATTN_TQ), F32),
                        pltpu.VMEM((2, ATTN_TK, ATTN_TQ), F32),
                        pltpu.VMEM((2, ATTN_TK, ATTN_TQ), BF16),
                        pltpu.VMEM((2, 1, ATTN_TQ), F32)],
        compiler_params=_cparams(("parallel", "parallel", "arbitrary")),
        name="attention",
    )(qt, k, vt)


def _premix_kernel(attn_ref, gb_ref, u_ref, up_ref, un_ref, cw_ref, ga_ref, gc_ref, o_ref, *, n_seq_tiles):
    i = pl.program_id(0)
    tm = u_ref.shape[0]
    o_ref[:, :ATTN_WIDTH] = (_rms(attn_ref[...].astype(F32)) * ga_ref[...]).astype(o_ref.dtype)

    u = u_ref[...].astype(F32)
    first = (i % n_seq_tiles) == 0
    last = (i % n_seq_tiles) == n_seq_tiles - 1
    prev_row = jnp.where(first, 0.0, up_ref[BF16_SUBLANES - 1:BF16_SUBLANES, :].astype(F32))
    next_row = jnp.where(last, 0.0, un_ref[0:1, :].astype(F32))
    row = lax.broadcasted_iota(jnp.int32, u.shape, 0)
    u_prev = jnp.where(row == 0, prev_row, pltpu.roll(u, 1, axis=0))
    u_next = jnp.where(row == tm - 1, next_row, pltpu.roll(u, tm - 1, axis=0))
    cw = cw_ref[...]
    conv = gb_ref[...].astype(F32) * (cw[0:1] * u_prev + cw[1:2] * u + cw[2:3] * u_next)
    o_ref[:, ATTN_WIDTH:] = (_rms(conv) * gc_ref[...]).astype(o_ref.dtype)


def premix(attn, gate_b, u, conv_w, g_attn, g_conv, seq_len, tm=ROW_TILE):
    t = attn.shape[0]
    tm = min(tm, seq_len)
    n_seq_tiles = seq_len // tm
    halo = tm // BF16_SUBLANES
    n_halo = t // BF16_SUBLANES
    row = lambda w: pl.BlockSpec((tm, w), lambda i: (i, 0))
    vec = lambda w: pl.BlockSpec((1, w), lambda i: (0, 0))
    return pl.pallas_call(
        functools.partial(_premix_kernel, n_seq_tiles=n_seq_tiles),
        grid=(t // tm,),
        in_specs=[row(ATTN_WIDTH), row(CONV_WIDTH), row(CONV_WIDTH),
                  pl.BlockSpec((BF16_SUBLANES, CONV_WIDTH), lambda i: (jnp.maximum(i * halo - 1, 0), 0)),
                  pl.BlockSpec((BF16_SUBLANES, CONV_WIDTH),
                               lambda i: (jnp.minimum((i + 1) * halo, n_halo - 1), 0)),
                  pl.BlockSpec((3, CONV_WIDTH), lambda i: (0, 0)),
                  vec(ATTN_WIDTH), vec(CONV_WIDTH)],
        out_specs=row(ATTN_WIDTH + CONV_WIDTH),
        out_shape=jax.ShapeDtypeStruct((t, ATTN_WIDTH + CONV_WIDTH), BF16),
        compiler_params=_cparams(("parallel",)),
        name="premix",
    )(attn, gate_b, u, u, u, conv_w, g_attn.reshape(1, -1), g_conv.reshape(1, -1))


def _rope_tables(seq_len):
    rows = seq_len // GRID_W
    inv = 1.0 / (ROPE_THETA ** (jnp.arange(0, ROPE_AXIS_DIM, 2, dtype=F32) / ROPE_AXIS_DIM))
    row_ang = jnp.arange(rows, dtype=F32)[:, None] * inv
    col_ang = jnp.arange(GRID_W, dtype=F32)[:, None] * inv
    ang = jnp.concatenate([
        jnp.broadcast_to(row_ang[:, None, :], (rows, GRID_W, inv.shape[0])),
        jnp.broadcast_to(col_ang[None, :, :], (rows, GRID_W, inv.shape[0])),
    ], axis=-1).reshape(seq_len, ROPE_AXIS_DIM)
    cos, sin = jnp.cos(ang), jnp.sin(ang)
    return jnp.concatenate([cos, cos], axis=-1), jnp.concatenate([-sin, sin], axis=-1)


def _prep_ffn(w_gate, w_up, w_down):
    return w_gate.astype(BF16), w_up.astype(BF16), w_down.astype(BF16)


def _ffn(hn, x, w, g_post, g_next):
    wg, wu, wd = w
    h = gate_up(hn, wg, wu)
    y = matmul(h, wd, *DOWN_TILE)
    return resid_norm_rows(y, x, g_post, FFN_RES_SCALE, g_next)


def _trunk(x3d, p):
    b, s, d = x3d.shape
    x = x3d.reshape(b * s, d)
    cos_full, sin_signed = _rope_tables(s)
    hn = rms_norm_rows(x, p["ffn1_pre"])
    x, hn = _ffn(hn, x, p["ffn1"], p["ffn1_post"], p["mix_pre"])
    qt, k, vt = inproj_qkv(hn, p["w_in"], p["q_norm"], p["k_norm"], cos_full, sin_signed, s)
    gate_b, u = inproj_conv(hn, p["w_in"])
    attn = attention(qt, k, vt, b, s)
    mix_in = premix(attn, gate_b, u, p["conv_w"], p["attn_out_norm"], p["conv_out_norm"], s)
    y = matmul(mix_in, p["w_out"], *OUT_PROJ_TILE)
    x, hn = resid_norm_rows(y, x, p["mix_post"], 1.0, p["ffn2_pre"])
    x, _ = _ffn(hn, x, p["ffn2"], p["ffn2_post"], None)
    return x.reshape(b, s, d)


def kernel(x_prompt, x_sample, ffn1_pre, ffn1_post, w1_gate, w1_up, w1_down, mix_pre, mix_post, w_in, q_norm, k_norm, conv_w, attn_out_norm, conv_out_norm, w_out, ffn2_pre, ffn2_post, w2_gate, w2_up, w2_down):
    p = {
        "ffn1_pre": ffn1_pre[0], "ffn1_post": ffn1_post[0],
        "ffn1": _prep_ffn(w1_gate[0], w1_up[0], w1_down[0]),
        "mix_pre": mix_pre[0], "mix_post": mix_post[0],
        "w_in": w_in[0].astype(BF16), "q_norm": q_norm[0], "k_norm": k_norm[0],
        "conv_w": conv_w[0], "attn_out_norm": attn_out_norm[0], "conv_out_norm": conv_out_norm[0],
        "w_out": w_out[0].astype(BF16),
        "ffn2_pre": ffn2_pre[0], "ffn2_post": ffn2_post[0],
        "ffn2": _prep_ffn(w2_gate[0], w2_up[0], w2_down[0]),
    }
    return _trunk(x_prompt, p), _trunk(x_sample, p)
```

```python
import functools

import jax
import jax.numpy as jnp
from jax import lax
from jax.experimental import pallas as pl
from jax.experimental.pallas import tpu as pltpu

F32 = jnp.float32
BF16 = jnp.bfloat16

NORM_EPS = 1e-6
FFN_RES_SCALE = 0.5
HEAD_DIM = 128
N_Q_HEADS = 16
N_KV_HEADS = 4
KV_GROUP = N_Q_HEADS // N_KV_HEADS
ATTN_WIDTH = N_Q_HEADS * HEAD_DIM
KV_WIDTH = N_KV_HEADS * HEAD_DIM
CONV_WIDTH = 2048
GRID_W = 64
ROPE_THETA = 10000.0
ROPE_AXIS_DIM = HEAD_DIM // 2

BF16_SUBLANES = 16
VMEM_LIMIT_BYTES = 56 * 1024 * 1024
GATE_UP_TILE = (4096, 256)
GATE_UP_SUB_ROWS = 1024
DOWN_TILE = (512, 512)
OUT_PROJ_TILE = (1024, 1024)
IN_PROJ_TM = 1024
ROW_TILE = 256


def _cparams(sem):
    return pltpu.CompilerParams(dimension_semantics=sem, vmem_limit_bytes=VMEM_LIMIT_BYTES)


def _rms(x):
    return x * lax.rsqrt(jnp.mean(x * x, axis=-1, keepdims=True) + NORM_EPS)


def _norm_kernel(x_ref, g_ref, o_ref):
    o_ref[...] = (_rms(x_ref[...]) * g_ref[...]).astype(o_ref.dtype)


def rms_norm_rows(x, g, tm=ROW_TILE):
    t, d = x.shape
    tm = min(tm, t)
    return pl.pallas_call(
        _norm_kernel,
        grid=(t // tm,),
        in_specs=[pl.BlockSpec((tm, d), lambda i: (i, 0)),
                  pl.BlockSpec((1, d), lambda i: (0, 0))],
        out_specs=pl.BlockSpec((tm, d), lambda i: (i, 0)),
        out_shape=jax.ShapeDtypeStruct((t, d), BF16),
        compiler_params=_cparams(("parallel",)),
        name="rms_norm_rows",
    )(x, g.reshape(1, d))


def _resid_norm_kernel(y_ref, x_ref, gp_ref, gn_ref, xo_ref, ho_ref, *, scale):
    x_new = x_ref[...] + scale * (_rms(y_ref[...]) * gp_ref[...])
    xo_ref[...] = x_new
    ho_ref[...] = (_rms(x_new) * gn_ref[...]).astype(ho_ref.dtype)


def _resid_kernel(y_ref, x_ref, gp_ref, xo_ref, *, scale):
    xo_ref[...] = x_ref[...] + scale * (_rms(y_ref[...]) * gp_ref[...])


def resid_norm_rows(y, x, g_post, scale, g_next=None, tm=ROW_TILE):
    t, d = x.shape
    tm = min(tm, t)
    row = pl.BlockSpec((tm, d), lambda i: (i, 0))
    vec = pl.BlockSpec((1, d), lambda i: (0, 0))
    if g_next is None:
        return pl.pallas_call(
            functools.partial(_resid_kernel, scale=scale),
            grid=(t // tm,),
            in_specs=[row, row, vec],
            out_specs=row,
            out_shape=jax.ShapeDtypeStruct((t, d), F32),
            compiler_params=_cparams(("parallel",)),
            name="resid_rows",
        )(y, x, g_post.reshape(1, d)), None
    return pl.pallas_call(
        functools.partial(_resid_norm_kernel, scale=scale),
        grid=(t // tm,),
        in_specs=[row, row, vec, vec],
        out_specs=[row, row],
        out_shape=[jax.ShapeDtypeStruct((t, d), F32), jax.ShapeDtypeStruct((t, d), BF16)],
        compiler_params=_cparams(("parallel",)),
        name="resid_norm_rows",
    )(y, x, g_post.reshape(1, d), g_next.reshape(1, d))


def _gateup_kernel(a_ref, wg_ref, wu_ref, o_ref):
    sub = min(GATE_UP_SUB_ROWS, a_ref.shape[0])
    for r in range(0, a_ref.shape[0], sub):
        a = a_ref[r:r + sub, :]
        g = jnp.dot(a, wg_ref[...], preferred_element_type=F32)
        u = jnp.dot(a, wu_ref[...], preferred_element_type=F32)
        o_ref[r:r + sub, :] = (g * jax.nn.sigmoid(g) * u).astype(o_ref.dtype)


def gate_up(a, wg, wu):
    t, d = a.shape
    f = wg.shape[1]
    tm, tn = min(GATE_UP_TILE[0], t), GATE_UP_TILE[1]
    return pl.pallas_call(
        _gateup_kernel,
        grid=(t // tm, f // tn),
        in_specs=[pl.BlockSpec((tm, d), lambda i, j: (i, 0), pipeline_mode=pl.Buffered(1)),
                  pl.BlockSpec((d, tn), lambda i, j: (0, j)),
                  pl.BlockSpec((d, tn), lambda i, j: (0, j))],
        out_specs=pl.BlockSpec((tm, tn), lambda i, j: (i, j)),
        out_shape=jax.ShapeDtypeStruct((t, f), BF16),
        compiler_params=_cparams(("parallel", "arbitrary")),
        name="gate_up",
    )(a, wg, wu)


def _mm_kernel(a_ref, w_ref, o_ref):
    o_ref[...] = jnp.dot(a_ref[...], w_ref[...], preferred_element_type=F32)


def matmul(a, w, tm, tn):
    t, kd = a.shape
    n = w.shape[1]
    tm, tn = min(tm, t), min(tn, n)
    return pl.pallas_call(
        _mm_kernel,
        grid=(t // tm, n // tn),
        in_specs=[pl.BlockSpec((tm, kd), lambda i, j: (i, 0)),
                  pl.BlockSpec((kd, tn), lambda i, j: (0, j))],
        out_specs=pl.BlockSpec((tm, tn), lambda i, j: (i, j)),
        out_shape=jax.ShapeDtypeStruct((t, n), F32),
        compiler_params=_cparams(("parallel", "arbitrary")),
        name="matmul_fullk",
    )(a, w)


QKV_TN = KV_GROUP * HEAD_DIM
N_Q_TILES = ATTN_WIDTH // QKV_TN
ATTN_TQ = 256
ATTN_TK = 1024
ATTN_SLAB = 64
ATTN_Q_TILES = 4
ATTN_CHUNK_UNROLL = 4
SCORE_BOUND_LOG2 = 40.0
Q_SCALE = (HEAD_DIM ** -0.5) * 1.4426950408889634


def _inproj_qkv_kernel(a_ref, w_ref, qg_ref, kg_ref, cos_ref, sin_ref, qt_ref, k_ref, vt_ref, raw_ref):
    j = pl.program_id(1)
    tm = a_ref.shape[0]
    n_steps = N_Q_TILES + 3

    def matmul(slot):
        raw_ref[slot] = jnp.dot(a_ref[...], w_ref[...], preferred_element_type=F32)

    def norm_rope(y, gain):
        y = _rms(y) * gain
        return y * cos_ref[...] + pltpu.roll(y, ROPE_AXIS_DIM, axis=1) * sin_ref[...]

    def store_transposed(dst_ref, h, y, width):
        yt = y.T.astype(dst_ref.dtype)
        for c in range(tm // width):
            dst_ref[h, c] = yt[:, c * width:(c + 1) * width]

    def epilogue(tile, slot):
        for h in range(KV_GROUP):
            y = raw_ref[slot, :, h * HEAD_DIM:(h + 1) * HEAD_DIM]
            if tile < N_Q_TILES:
                store_transposed(qt_ref, h, norm_rope(y, qg_ref[...]) * Q_SCALE, ATTN_TQ)
            elif tile == N_Q_TILES:
                k_ref[h] = norm_rope(y, kg_ref[...]).astype(k_ref.dtype)
            else:
                store_transposed(vt_ref, h, y, ATTN_TK)

    for step in range(n_steps):
        @pl.when(j == step)
        def _(step=step):
            if step < n_steps - 1:
                matmul(step % 2)
            if step > 0:
                epilogue(step - 1, (step - 1) % 2)


def inproj_qkv(a, w_in, q_norm, k_norm, cos_full, sin_signed, seq_len, tm=IN_PROJ_TM):
    t, d = a.shape
    tm = min(tm, seq_len)
    n_seq_tiles = seq_len // tm
    return pl.pallas_call(
        _inproj_qkv_kernel,
        grid=(t // tm, N_Q_TILES + 3),
        in_specs=[pl.BlockSpec((tm, d), lambda i, j: (i, 0)),
                  pl.BlockSpec((d, QKV_TN), lambda i, j: (0, jnp.minimum(j, N_Q_TILES + 1))),
                  pl.BlockSpec((1, HEAD_DIM), lambda i, j: (0, 0)),
                  pl.BlockSpec((1, HEAD_DIM), lambda i, j: (0, 0)),
                  pl.BlockSpec((tm, HEAD_DIM), lambda i, j: (i % n_seq_tiles, 0)),
                  pl.BlockSpec((tm, HEAD_DIM), lambda i, j: (i % n_seq_tiles, 0))],
        out_specs=[pl.BlockSpec((KV_GROUP, tm // ATTN_TQ, HEAD_DIM, ATTN_TQ),
                                lambda i, j: (jnp.clip(j - 1, 0, N_Q_TILES - 1), i, 0, 0)),
                   pl.BlockSpec((N_KV_HEADS, tm, HEAD_DIM), lambda i, j: (0, i, 0)),
                   pl.BlockSpec((N_KV_HEADS, tm // ATTN_TK, HEAD_DIM, ATTN_TK), lambda i, j: (0, i, 0, 0))],
        out_shape=[jax.ShapeDtypeStruct((N_Q_HEADS, t // ATTN_TQ, HEAD_DIM, ATTN_TQ), BF16),
                   jax.ShapeDtypeStruct((N_KV_HEADS, t, HEAD_DIM), BF16),
                   jax.ShapeDtypeStruct((N_KV_HEADS, t // ATTN_TK, HEAD_DIM, ATTN_TK), BF16)],
        scratch_shapes=[pltpu.VMEM((2, tm, QKV_TN), F32)],
        compiler_params=_cparams(("parallel", "arbitrary")),
        name="inproj_qkv",
    )(a, w_in, q_norm.reshape(1, HEAD_DIM), k_norm.reshape(1, HEAD_DIM), cos_full, sin_signed)


def _inproj_conv_kernel(a_ref, wb_ref, wc_ref, wh_ref, gb_ref, u_ref):
    a = a_ref[...]
    gb_ref[...] = jnp.dot(a, wb_ref[...], preferred_element_type=F32).astype(gb_ref.dtype)
    c = jnp.dot(a, wc_ref[...], preferred_element_type=F32)
    h = jnp.dot(a, wh_ref[...], preferred_element_type=F32)
    u_ref[...] = (c * h).astype(u_ref.dtype)


def inproj_conv(a, w_in, tm=IN_PROJ_TM, tn=QKV_TN):
    t, d = a.shape
    tm = min(tm, t)
    off_b = (ATTN_WIDTH + 2 * KV_WIDTH) // tn
    off_c = off_b + CONV_WIDTH // tn
    off_h = off_c + CONV_WIDTH // tn
    out = jax.ShapeDtypeStruct((t, CONV_WIDTH), BF16)
    return pl.pallas_call(
        _inproj_conv_kernel,
        grid=(t // tm, CONV_WIDTH // tn),
        in_specs=[pl.BlockSpec((tm, d), lambda i, j: (i, 0)),
                  pl.BlockSpec((d, tn), lambda i, j: (0, off_b + j)),
                  pl.BlockSpec((d, tn), lambda i, j: (0, off_c + j)),
                  pl.BlockSpec((d, tn), lambda i, j: (0, off_h + j))],
        out_specs=[pl.BlockSpec((tm, tn), lambda i, j: (i, j)),
                   pl.BlockSpec((tm, tn), lambda i, j: (i, j))],
        out_shape=[out, out],
        compiler_params=_cparams(("parallel", "arbitrary")),
        name="inproj_conv",
    )(a, w_in, w_in, w_in)


def _attn_kernel(qt_ref, k_ref, vt_ref, o_ref, ksq_ref, m_ref, l_ref, acc_ref, s_ref, p_ref, alpha_ref):
    group, n_q_tiles, _, tq = qt_ref.shape
    n_heads = group * n_q_tiles
    n_chunks = vt_ref.shape[1]
    tk = vt_ref.shape[3]
    slabs = [pl.ds(r, ATTN_SLAB) for r in range(0, tk, ATTN_SLAB)]

    def q_tile(h):
        return qt_ref[h % group, h // group]

    def key_chunk(c):
        return k_ref[0, pl.ds(pl.multiple_of(c * tk, tk), tk), :]

    @pl.when(pl.program_id(2) == 0)
    def _():
        def chunk_max(c, best):
            kc = key_chunk(c).astype(F32)
            return jnp.maximum(best, jnp.max(jnp.sum(kc * kc, axis=1, keepdims=True)))
        ksq_ref[0] = lax.fori_loop(0, n_chunks, chunk_max, jnp.float32(0.0))

    qsq = jnp.float32(0.0)
    for h in range(n_heads):
        q = q_tile(h).astype(F32)
        qsq = jnp.maximum(qsq, jnp.max(jnp.sum(q * q, axis=0, keepdims=True)))
    scores_bounded = qsq * ksq_ref[0] <= SCORE_BOUND_LOG2 * SCORE_BOUND_LOG2

    l_ref[...] = jnp.zeros(l_ref.shape, F32)
    acc_ref[...] = jnp.zeros(acc_ref.shape, F32)
    p_ref[...] = jnp.zeros(p_ref.shape, p_ref.dtype)

    def pipeline(stage_scores, stage_values, unroll):
        unroll = unroll if n_chunks % unroll == 0 else 1

        @pl.loop(0, n_chunks // unroll)
        def _(step):
            for sub in range(unroll):
                c = step * unroll + sub
                for h in range(n_heads):
                    slot, other = h % 2, (h + 1) % 2
                    stage_scores(c, h, slot)
                    if h > 0:
                        stage_values(c, h - 1, other)
                    else:
                        stage_values(jnp.maximum(c - 1, 0), n_heads - 1, other)
        stage_values(n_chunks - 1, n_heads - 1, (n_heads - 1) % 2)

    def unshifted():
        def probs(c, h, slot):
            s = jnp.dot(key_chunk(c), q_tile(h), preferred_element_type=F32)
            slab_sum = jnp.zeros((ATTN_SLAB, s.shape[1]), F32)
            for r in range(0, tk, ATTN_SLAB):
                p = jnp.exp2(s[r:r + ATTN_SLAB])
                slab_sum = slab_sum + p
                p_ref[slot, pl.ds(r, ATTN_SLAB), :] = p.astype(p_ref.dtype)
            l_ref[h] = l_ref[h] + jnp.sum(slab_sum, axis=0, keepdims=True)

        def values(c, h, slot):
            acc_ref[h] = acc_ref[h] + jnp.dot(vt_ref[0, c], p_ref[slot], preferred_element_type=F32)

        pipeline(probs, values, ATTN_CHUNK_UNROLL)

    def running_max():
        def scores(c, h, slot):
            s_ref[slot] = jnp.dot(key_chunk(c), q_tile(h), preferred_element_type=F32)

        def softmax(h, slot):
            m_old = m_ref[h]
            slab_max = s_ref[slot, slabs[0], :]
            for rows in slabs[1:]:
                slab_max = jnp.maximum(slab_max, s_ref[slot, rows, :])
            m_new = jnp.maximum(m_old, jnp.max(slab_max, axis=0, keepdims=True))
            alpha = jnp.exp2(m_old - m_new)
            slab_sum = jnp.zeros(slab_max.shape, F32)
            for rows in slabs:
                p = jnp.exp2(s_ref[slot, rows, :] - m_new)
                slab_sum = slab_sum + p
                p_ref[slot, rows, :] = p.astype(p_ref.dtype)
            l_ref[h] = alpha * l_ref[h] + jnp.sum(slab_sum, axis=0, keepdims=True)
            m_ref[h] = m_new
            alpha_ref[slot] = alpha

        def scores_and_softmax(c, h, slot):
            if h + 1 < n_heads:
                scores(c, h + 1, (h + 1) % 2)
            else:
                scores(jnp.minimum(c + 1, n_chunks - 1), 0, (h + 1) % 2)
            softmax(h, slot)

        def values(c, h, slot):
            pv = jnp.dot(vt_ref[0, c], p_ref[slot], preferred_element_type=F32)
            acc_ref[h] = alpha_ref[slot] * acc_ref[h] + pv

        m_ref[...] = jnp.full(m_ref.shape, -jnp.inf, F32)
        alpha_ref[...] = jnp.ones(alpha_ref.shape, F32)
        scores(0, 0, 0)
        pipeline(scores_and_softmax, values, 1)

    lax.cond(scores_bounded, unshifted, running_max)

    for h in range(n_heads):
        out_t = acc_ref[h] / l_ref[h]
        head, tile = h % group, h // group
        o_ref[tile * tq:(tile + 1) * tq, head * HEAD_DIM:(head + 1) * HEAD_DIM] = out_t.T.astype(o_ref.dtype)


def attention(qt, k, vt, batch, seq_len):
    t = k.shape[1]
    q_tiles = min(ATTN_Q_TILES, seq_len // ATTN_TQ)
    nq = seq_len // (ATTN_TQ * q_tiles)
    nk = seq_len // ATTN_TK
    n_units = KV_GROUP * q_tiles
    return pl.pallas_call(
        _attn_kernel,
        grid=(batch, N_KV_HEADS, nq),
        in_specs=[pl.BlockSpec((KV_GROUP, q_tiles, HEAD_DIM, ATTN_TQ), lambda b, g, i: (g, b * nq + i, 0, 0)),
                  pl.BlockSpec((1, seq_len, HEAD_DIM), lambda b, g, i: (g, b, 0)),
                  pl.BlockSpec((1, nk, HEAD_DIM, ATTN_TK), lambda b, g, i: (g, b, 0, 0))],
        out_specs=pl.BlockSpec((ATTN_TQ * q_tiles, KV_GROUP * HEAD_DIM), lambda b, g, i: (b * nq + i, g)),
        out_shape=jax.ShapeDtypeStruct((t, ATTN_WIDTH), BF16),
        scratch_shapes=[pltpu.SMEM((1,), F32),
                        pltpu.VMEM((n_units, 1, ATTN_TQ), F32),
                        pltpu.VMEM((n_units, 1, ATTN_TQ), F32),
                        pltpu.VMEM((n_units, HEAD_DIM, ATTN_TQ), F32),
                        pltpu.VMEM((2, ATTN_TK, ATTN_TQ), F32),
                        pltpu.VMEM((2, ATTN_TK, ATTN_TQ), BF16),
                        pltpu.VMEM((2, 1, ATTN_TQ), F32)],
        compiler_params=_cparams(("parallel", "parallel", "arbitrary")),
        name="attention",
    )(qt, k, vt)


def _premix_kernel(attn_ref, gb_ref, u_ref, up_ref, un_ref, cw_ref, ga_ref, gc_ref, o_ref, *, n_seq_tiles):
    i = pl.program_id(0)
    tm = u_ref.shape[0]
    o_ref[:, :ATTN_WIDTH] = (_rms(attn_ref[...].astype(F32)) * ga_ref[...]).astype(o_ref.dtype)

    u = u_ref[...].astype(F32)
    first = (i % n_seq_tiles) == 0
    last = (i % n_seq_tiles) == n_seq_tiles - 1
    prev_row = jnp.where(first, 0.0, up_ref[BF16_SUBLANES - 1:BF16_SUBLANES, :].astype(F32))
    next_row = jnp.where(last, 0.0, un_ref[0:1, :].astype(F32))
    row = lax.broadcasted_iota(jnp.int32, u.shape, 0)
    u_prev = jnp.where(row == 0, prev_row, pltpu.roll(u, 1, axis=0))
    u_next = jnp.where(row == tm - 1, next_row, pltpu.roll(u, tm - 1, axis=0))
    cw = cw_ref[...]
    conv = gb_ref[...].astype(F32) * (cw[0:1] * u_prev + cw[1:2] * u + cw[2:3] * u_next)
    o_ref[:, ATTN_WIDTH:] = (_rms(conv) * gc_ref[...]).astype(o_ref.dtype)


def premix(attn, gate_b, u, conv_w, g_attn, g_conv, seq_len, tm=ROW_TILE):
    t = attn.shape[0]
    tm = min(tm, seq_len)
    n_seq_tiles = seq_len // tm
    halo = tm // BF16_SUBLANES
    n_halo = t // BF16_SUBLANES
    row = lambda w: pl.BlockSpec((tm, w), lambda i: (i, 0))
    vec = lambda w: pl.BlockSpec((1, w), lambda i: (0, 0))
    return pl.pallas_call(
        functools.partial(_premix_kernel, n_seq_tiles=n_seq_tiles),
        grid=(t // tm,),
        in_specs=[row(ATTN_WIDTH), row(CONV_WIDTH), row(CONV_WIDTH),
                  pl.BlockSpec((BF16_SUBLANES, CONV_WIDTH), lambda i: (jnp.maximum(i * halo - 1, 0), 0)),
                  pl.BlockSpec((BF16_SUBLANES, CONV_WIDTH),
                               lambda i: (jnp.minimum((i + 1) * halo, n_halo - 1), 0)),
                  pl.BlockSpec((3, CONV_WIDTH), lambda i: (0, 0)),
                  vec(ATTN_WIDTH), vec(CONV_WIDTH)],
        out_specs=row(ATTN_WIDTH + CONV_WIDTH),
        out_shape=jax.ShapeDtypeStruct((t, ATTN_WIDTH + CONV_WIDTH), BF16),
        compiler_params=_cparams(("parallel",)),
        name="premix",
    )(attn, gate_b, u, u, u, conv_w, g_attn.reshape(1, -1), g_conv.reshape(1, -1))


def _rope_tables(seq_len):
    rows = seq_len // GRID_W
    inv = 1.0 / (ROPE_THETA ** (jnp.arange(0, ROPE_AXIS_DIM, 2, dtype=F32) / ROPE_AXIS_DIM))
    row_ang = jnp.arange(rows, dtype=F32)[:, None] * inv
    col_ang = jnp.arange(GRID_W, dtype=F32)[:, None] * inv
    ang = jnp.concatenate([
        jnp.broadcast_to(row_ang[:, None, :], (rows, GRID_W, inv.shape[0])),
        jnp.broadcast_to(col_ang[None, :, :], (rows, GRID_W, inv.shape[0])),
    ], axis=-1).reshape(seq_len, ROPE_AXIS_DIM)
    cos, sin = jnp.cos(ang), jnp.sin(ang)
    return jnp.concatenate([cos, cos], axis=-1), jnp.concatenate([-sin, sin], axis=-1)


def _prep_ffn(w_gate, w_up, w_down):
    return w_gate.astype(BF16), w_up.astype(BF16), w_down.astype(BF16)


def _ffn(hn, x, w, g_post, g_next):
    wg, wu, wd = w
    h = gate_up(hn, wg, wu)
    y = matmul(h, wd, *DOWN_TILE)
    return resid_norm_rows(y, x, g_post, FFN_RES_SCALE, g_next)


def _trunk(x3d, p):
    b, s, d = x3d.shape
    x = x3d.reshape(b * s, d)
    cos_full, sin_signed = _rope_tables(s)
    hn = rms_norm_rows(x, p["ffn1_pre"])
    x, hn = _ffn(hn, x, p["ffn1"], p["ffn1_post"], p["mix_pre"])
    qt, k, vt = inproj_qkv(hn, p["w_in"], p["q_norm"], p["k_norm"], cos_full, sin_signed, s)
    gate_b, u = inproj_conv(hn, p["w_in"])
    attn = attention(qt, k, vt, b, s)
    mix_in = premix(attn, gate_b, u, p["conv_w"], p["attn_out_norm"], p["conv_out_norm"], s)
    y = matmul(mix_in, p["w_out"], *OUT_PROJ_TILE)
    x, hn = resid_norm_rows(y, x, p["mix_post"], 1.0, p["ffn2_pre"])
    x, _ = _ffn(hn, x, p["ffn2"], p["ffn2_post"], None)
    return x.reshape(b, s, d)


def kernel(x_prompt, x_sample, ffn1_pre, ffn1_post, w1_gate, w1_up, w1_down, mix_pre, mix_post, w_in, q_norm, k_norm, conv_w, attn_out_norm, conv_out_norm, w_out, ffn2_pre, ffn2_post, w2_gate, w2_up, w2_down):
    p = {
        "ffn1_pre": ffn1_pre[0], "ffn1_post": ffn1_post[0],
        "ffn1": _prep_ffn(w1_gate[0], w1_up[0], w1_down[0]),
        "mix_pre": mix_pre[0], "mix_post": mix_post[0],
        "w_in": w_in[0].astype(BF16), "q_norm": q_norm[0], "k_norm": k_norm[0],
        "conv_w": conv_w[0], "attn_out_norm": attn_out_norm[0], "conv_out_norm": conv_out_norm[0],
        "w_out": w_out[0].astype(BF16),
        "ffn2_pre": ffn2_pre[0], "ffn2_post": ffn2_post[0],
        "ffn2": _prep_ffn(w2_gate[0], w2_up[0], w2_down[0]),
    }
    return _trunk(x_prompt, p), _trunk(x_sample, p)
```

```python
import functools

import jax
import jax.numpy as jnp
from jax import lax
from jax.experimental import pallas as pl
from jax.experimental.pallas import tpu as pltpu

F32 = jnp.float32
BF16 = jnp.bfloat16

NORM_EPS = 1e-6
FFN_RES_SCALE = 0.5
HEAD_DIM = 128
N_Q_HEADS = 16
N_KV_HEADS = 4
KV_GROUP = N_Q_HEADS // N_KV_HEADS
ATTN_WIDTH = N_Q_HEADS * HEAD_DIM
KV_WIDTH = N_KV_HEADS * HEAD_DIM
CONV_WIDTH = 2048
GRID_W = 64
ROPE_THETA = 10000.0
ROPE_AXIS_DIM = HEAD_DIM // 2

BF16_SUBLANES = 16
VMEM_LIMIT_BYTES = 56 * 1024 * 1024
GATE_UP_TILE = (4096, 256)
GATE_UP_SUB_ROWS = 1024
DOWN_TILE = (512, 512)
OUT_PROJ_TILE = (1024, 1024)
IN_PROJ_TM = 1024
ROW_TILE = 256


def _cparams(sem):
    return pltpu.CompilerParams(dimension_semantics=sem, vmem_limit_bytes=VMEM_LIMIT_BYTES)


def _rms(x):
    return x * lax.rsqrt(jnp.mean(x * x, axis=-1, keepdims=True) + NORM_EPS)


def _norm_kernel(x_ref, g_ref, o_ref):
    o_ref[...] = (_rms(x_ref[...]) * g_ref[...]).astype(o_ref.dtype)


def rms_norm_rows(x, g, tm=ROW_TILE):
    t, d = x.shape
    tm = min(tm, t)
    return pl.pallas_call(
        _norm_kernel,
        grid=(t // tm,),
        in_specs=[pl.BlockSpec((tm, d), lambda i: (i, 0)),
                  pl.BlockSpec((1, d), lambda i: (0, 0))],
        out_specs=pl.BlockSpec((tm, d), lambda i: (i, 0)),
        out_shape=jax.ShapeDtypeStruct((t, d), BF16),
        compiler_params=_cparams(("parallel",)),
        name="rms_norm_rows",
    )(x, g.reshape(1, d))


def _resid_norm_kernel(y_ref, x_ref, gp_ref, gn_ref, xo_ref, ho_ref, *, scale):
    x_new = x_ref[...] + scale * (_rms(y_ref[...]) * gp_ref[...])
    xo_ref[...] = x_new
    ho_ref[...] = (_rms(x_new) * gn_ref[...]).astype(ho_ref.dtype)


def _resid_kernel(y_ref, x_ref, gp_ref, xo_ref, *, scale):
    xo_ref[...] = x_ref[...] + scale * (_rms(y_ref[...]) * gp_ref[...])


def resid_norm_rows(y, x, g_post, scale, g_next=None, tm=ROW_TILE):
    t, d = x.shape
    tm = min(tm, t)
    row = pl.BlockSpec((tm, d), lambda i: (i, 0))
    vec = pl.BlockSpec((1, d), lambda i: (0, 0))
    if g_next is None:
        return pl.pallas_call(
            functools.partial(_resid_kernel, scale=scale),
            grid=(t // tm,),
            in_specs=[row, row, vec],
            out_specs=row,
            out_shape=jax.ShapeDtypeStruct((t, d), F32),
            compiler_params=_cparams(("parallel",)),
            name="resid_rows",
        )(y, x, g_post.reshape(1, d)), None
    return pl.pallas_call(
        functools.partial(_resid_norm_kernel, scale=scale),
        grid=(t // tm,),
        in_specs=[row, row, vec, vec],
        out_specs=[row, row],
        out_shape=[jax.ShapeDtypeStruct((t, d), F32), jax.ShapeDtypeStruct((t, d), BF16)],
        compiler_params=_cparams(("parallel",)),
        name="resid_norm_rows",
    )(y, x, g_post.reshape(1, d), g_next.reshape(1, d))


def _gateup_kernel(a_ref, wg_ref, wu_ref, o_ref):
    sub = min(GATE_UP_SUB_ROWS, a_ref.shape[0])
    wg = wg_ref[...].astype(a_ref.dtype)
    wu = wu_ref[...].astype(a_ref.dtype)
    for r in range(0, a_ref.shape[0], sub):
        a = a_ref[r:r + sub, :]
        g = jnp.dot(a, wg, preferred_element_type=F32)
        u = jnp.dot(a, wu, preferred_element_type=F32)
        o_ref[r:r + sub, :] = (g * jax.nn.sigmoid(g) * u).astype(o_ref.dtype)


def gate_up(a, wg, wu):
    t, d = a.shape
    f = wg.shape[1]
    tm, tn = min(GATE_UP_TILE[0], t), GATE_UP_TILE[1]
    return pl.pallas_call(
        _gateup_kernel,
        grid=(t // tm, f // tn),
        in_specs=[pl.BlockSpec((tm, d), lambda i, j: (i, 0), pipeline_mode=pl.Buffered(1)),
                  pl.BlockSpec((d, tn), lambda i, j: (0, j)),
                  pl.BlockSpec((d, tn), lambda i, j: (0, j))],
        out_specs=pl.BlockSpec((tm, tn), lambda i, j: (i, j)),
        out_shape=jax.ShapeDtypeStruct((t, f), BF16),
        compiler_params=_cparams(("parallel", "arbitrary")),
        name="gate_up",
    )(a, wg, wu)


def _mm_kernel(a_ref, w_ref, o_ref):
    o_ref[...] = jnp.dot(a_ref[...], w_ref[...], preferred_element_type=F32)


def matmul(a, w, tm, tn):
    t, kd = a.shape
    n = w.shape[1]
    tm, tn = min(tm, t), min(tn, n)
    return pl.pallas_call(
        _mm_kernel,
        grid=(t // tm, n // tn),
        in_specs=[pl.BlockSpec((tm, kd), lambda i, j: (i, 0)),
                  pl.BlockSpec((kd, tn), lambda i, j: (0, j))],
        out_specs=pl.BlockSpec((tm, tn), lambda i, j: (i, j)),
        out_shape=jax.ShapeDtypeStruct((t, n), F32),
        compiler_params=_cparams(("parallel", "arbitrary")),
        name="matmul_fullk",
    )(a, w)


QKV_TN = KV_GROUP * HEAD_DIM
N_Q_TILES = ATTN_WIDTH // QKV_TN
ATTN_TQ = 256
ATTN_TK = 1024
ATTN_SLAB = 64
ATTN_Q_TILES = 4
ATTN_CHUNK_UNROLL = 4
SCORE_BOUND_LOG2 = 40.0
Q_SCALE = (HEAD_DIM ** -0.5) * 1.4426950408889634


def _inproj_qkv_kernel(a_ref, w_ref, qg_ref, kg_ref, cos_ref, sin_ref, qt_ref, k_ref, vt_ref, raw_ref):
    j = pl.program_id(1)
    tm = a_ref.shape[0]
    n_steps = N_Q_TILES + 3

    def matmul(slot):
        raw_ref[slot] = jnp.dot(a_ref[...], w_ref[...], preferred_element_type=F32)

    def norm_rope(y, gain):
        y = _rms(y) * gain
        return y * cos_ref[...] + pltpu.roll(y, ROPE_AXIS_DIM, axis=1) * sin_ref[...]

    def store_transposed(dst_ref, h, y, width):
        yt = y.T.astype(dst_ref.dtype)
        for c in range(tm // width):
            dst_ref[h, c] = yt[:, c * width:(c + 1) * width]

    def epilogue(tile, slot):
        for h in range(KV_GROUP):
            y = raw_ref[slot, :, h * HEAD_DIM:(h + 1) * HEAD_DIM]
            if tile < N_Q_TILES:
                store_transposed(qt_ref, h, norm_rope(y, qg_ref[...]) * Q_SCALE, ATTN_TQ)
            elif tile == N_Q_TILES:
                k_ref[h] = norm_rope(y, kg_ref[...]).astype(k_ref.dtype)
            else:
                store_transposed(vt_ref, h, y, ATTN_TK)

    for step in range(n_steps):
        @pl.when(j == step)
        def _(step=step):
            if step < n_steps - 1:
                matmul(step % 2)
            if step > 0:
                epilogue(step - 1, (step - 1) % 2)


def inproj_qkv(a, w_in, q_norm, k_norm, cos_full, sin_signed, seq_len, tm=IN_PROJ_TM):
    t, d = a.shape
    tm = min(tm, seq_len)
    n_seq_tiles = seq_len // tm
    return pl.pallas_call(
        _inproj_qkv_kernel,
        grid=(t // tm, N_Q_TILES + 3),
        in_specs=[pl.BlockSpec((tm, d), lambda i, j: (i, 0)),
                  pl.BlockSpec((d, QKV_TN), lambda i, j: (0, jnp.minimum(j, N_Q_TILES + 1))),
                  pl.BlockSpec((1, HEAD_DIM), lambda i, j: (0, 0)),
                  pl.BlockSpec((1, HEAD_DIM), lambda i, j: (0, 0)),
                  pl.BlockSpec((tm, HEAD_DIM), lambda i, j: (i % n_seq_tiles, 0)),
                  pl.BlockSpec((tm, HEAD_DIM), lambda i, j: (i % n_seq_tiles, 0))],
        out_specs=[pl.BlockSpec((KV_GROUP, tm // ATTN_TQ, HEAD_DIM, ATTN_TQ),
                                lambda i, j: (jnp.clip(j - 1, 0, N_Q_TILES - 1), i, 0, 0)),
                   pl.BlockSpec((N_KV_HEADS, tm, HEAD_DIM), lambda i, j: (0, i, 0)),
                   pl.BlockSpec((N_KV_HEADS, tm // ATTN_TK, HEAD_DIM, ATTN_TK), lambda i, j: (0, i, 0, 0))],
        out_shape=[jax.ShapeDtypeStruct((N_Q_HEADS, t // ATTN_TQ, HEAD_DIM, ATTN_TQ), BF16),
                   jax.ShapeDtypeStruct((N_KV_HEADS, t, HEAD_DIM), BF16),
                   jax.ShapeDtypeStruct((N_KV_HEADS, t // ATTN_TK, HEAD_DIM, ATTN_TK), BF16)],
        scratch_shapes=[pltpu.VMEM((2, tm, QKV_TN), F32)],
        compiler_params=_cparams(("parallel", "arbitrary")),
        name="inproj_qkv",
    )(a, w_in, q_norm.reshape(1, HEAD_DIM), k_norm.reshape(1, HEAD_DIM), cos_full, sin_signed)


def _inproj_conv_kernel(a_ref, wb_ref, wc_ref, wh_ref, gb_ref, u_ref):
    a = a_ref[...]
    gb_ref[...] = jnp.dot(a, wb_ref[...], preferred_element_type=F32).astype(gb_ref.dtype)
    c = jnp.dot(a, wc_ref[...], preferred_element_type=F32)
    h = jnp.dot(a, wh_ref[...], preferred_element_type=F32)
    u_ref[...] = (c * h).astype(u_ref.dtype)


def inproj_conv(a, w_in, tm=IN_PROJ_TM, tn=QKV_TN):
    t, d = a.shape
    tm = min(tm, t)
    off_b = (ATTN_WIDTH + 2 * KV_WIDTH) // tn
    off_c = off_b + CONV_WIDTH // tn
    off_h = off_c + CONV_WIDTH // tn
    out = jax.ShapeDtypeStruct((t, CONV_WIDTH), BF16)
    return pl.pallas_call(
        _inproj_conv_kernel,
        grid=(t // tm, CONV_WIDTH // tn),
        in_specs=[pl.BlockSpec((tm, d), lambda i, j: (i, 0)),
                  pl.BlockSpec((d, tn), lambda i, j: (0, off_b + j)),
                  pl.BlockSpec((d, tn), lambda i, j: (0, off_c + j)),
                  pl.BlockSpec((d, tn), lambda i, j: (0, off_h + j))],
        out_specs=[pl.BlockSpec((tm, tn), lambda i, j: (i, j)),
                   pl.BlockSpec((tm, tn), lambda i, j: (i, j))],
        out_shape=[out, out],
        compiler_params=_cparams(("parallel", "arbitrary")),
        name="inproj_conv",
    )(a, w_in, w_in, w_in)


def _attn_kernel(qt_ref, k_ref, vt_ref, o_ref, ksq_ref, m_ref, l_ref, acc_ref, s_ref, p_ref, alpha_ref):
    group, n_q_tiles, _, tq = qt_ref.shape
    n_heads = group * n_q_tiles
    n_chunks = vt_ref.shape[1]
    tk = vt_ref.shape[3]
    slabs = [pl.ds(r, ATTN_SLAB) for r in range(0, tk, ATTN_SLAB)]

    def q_tile(h):
        return qt_ref[h % group, h // group]

    def key_chunk(c):
        return k_ref[0, pl.ds(pl.multiple_of(c * tk, tk), tk), :]

    @pl.when(pl.program_id(2) == 0)
    def _():
        def chunk_max(c, best):
            kc = key_chunk(c).astype(F32)
            return jnp.maximum(best, jnp.max(jnp.sum(kc * kc, axis=1, keepdims=True)))
        ksq_ref[0] = lax.fori_loop(0, n_chunks, chunk_max, jnp.float32(0.0))

    qsq = jnp.float32(0.0)
    for h in range(n_heads):
        q = q_tile(h).astype(F32)
        qsq = jnp.maximum(qsq, jnp.max(jnp.sum(q * q, axis=0, keepdims=True)))
    scores_bounded = qsq * ksq_ref[0] <= SCORE_BOUND_LOG2 * SCORE_BOUND_LOG2

    l_ref[...] = jnp.zeros(l_ref.shape, F32)
    acc_ref[...] = jnp.zeros(acc_ref.shape, F32)
    p_ref[...] = jnp.zeros(p_ref.shape, p_ref.dtype)

    def pipeline(stage_scores, stage_values, unroll):
        unroll = unroll if n_chunks % unroll == 0 else 1

        @pl.loop(0, n_chunks // unroll)
        def _(step):
            for sub in range(unroll):
                c = step * unroll + sub
                for h in range(n_heads):
                    slot, other = h % 2, (h + 1) % 2
                    stage_scores(c, h, slot)
                    if h > 0:
                        stage_values(c, h - 1, other)
                    else:
                        stage_values(jnp.maximum(c - 1, 0), n_heads - 1, other)
        stage_values(n_chunks - 1, n_heads - 1, (n_heads - 1) % 2)

    def unshifted():
        def probs(c, h, slot):
            s = jnp.dot(key_chunk(c), q_tile(h), preferred_element_type=F32)
            slab_sum = jnp.zeros((ATTN_SLAB, s.shape[1]), F32)
            for r in range(0, tk, ATTN_SLAB):
                p = jnp.exp2(s[r:r + ATTN_SLAB])
                slab_sum = slab_sum + p
                p_ref[slot, pl.ds(r, ATTN_SLAB), :] = p.astype(p_ref.dtype)
            l_ref[h] = l_ref[h] + jnp.sum(slab_sum, axis=0, keepdims=True)

        def values(c, h, slot):
            acc_ref[h] = acc_ref[h] + jnp.dot(vt_ref[0, c], p_ref[slot], preferred_element_type=F32)

        pipeline(probs, values, ATTN_CHUNK_UNROLL)

    def running_max():
        def scores(c, h, slot):
            s_ref[slot] = jnp.dot(key_chunk(c), q_tile(h), preferred_element_type=F32)

        def softmax(h, slot):
            m_old = m_ref[h]
            slab_max = s_ref[slot, slabs[0], :]
            for rows in slabs[1:]:
                slab_max = jnp.maximum(slab_max, s_ref[slot, rows, :])
            m_new = jnp.maximum(m_old, jnp.max(slab_max, axis=0, keepdims=True))
            alpha = jnp.exp2(m_old - m_new)
            slab_sum = jnp.zeros(slab_max.shape, F32)
            for rows in slabs:
                p = jnp.exp2(s_ref[slot, rows, :] - m_new)
                slab_sum = slab_sum + p
                p_ref[slot, rows, :] = p.astype(p_ref.dtype)
            l_ref[h] = alpha * l_ref[h] + jnp.sum(slab_sum, axis=0, keepdims=True)
            m_ref[h] = m_new
            alpha_ref[slot] = alpha

        def scores_and_softmax(c, h, slot):
            if h + 1 < n_heads:
                scores(c, h + 1, (h + 1) % 2)
            else:
                scores(jnp.minimum(c + 1, n_chunks - 1), 0, (h + 1) % 2)
            softmax(h, slot)

        def values(c, h, slot):
            pv = jnp.dot(vt_ref[0, c], p_ref[slot], preferred_element_type=F32)
            acc_ref[h] = alpha_ref[slot] * acc_ref[h] + pv

        m_ref[...] = jnp.full(m_ref.shape, -jnp.inf, F32)
        alpha_ref[...] = jnp.ones(alpha_ref.shape, F32)
        scores(0, 0, 0)
        pipeline(scores_and_softmax, values, 1)

    lax.cond(scores_bounded, unshifted, running_max)

    for h in range(n_heads):
        out_t = acc_ref[h] / l_ref[h]
        head, tile = h % group, h // group
        o_ref[tile * tq:(tile + 1) * tq, head * HEAD_DIM:(head + 1) * HEAD_DIM] = out_t.T.astype(o_ref.dtype)


def attention(qt, k, vt, batch, seq_len):
    t = k.shape[1]
    q_tiles = min(ATTN_Q_TILES, seq_len // ATTN_TQ)
    nq = seq_len // (ATTN_TQ * q_tiles)
    nk = seq_len // ATTN_TK
    n_units = KV_GROUP * q_tiles
    return pl.pallas_call(
        _attn_kernel,
        grid=(batch, N_KV_HEADS, nq),
        in_specs=[pl.BlockSpec((KV_GROUP, q_tiles, HEAD_DIM, ATTN_TQ), lambda b, g, i: (g, b * nq + i, 0, 0)),
                  pl.BlockSpec((1, seq_len, HEAD_DIM), lambda b, g, i: (g, b, 0)),
                  pl.BlockSpec((1, nk, HEAD_DIM, ATTN_TK), lambda b, g, i: (g, b, 0, 0))],
        out_specs=pl.BlockSpec((ATTN_TQ * q_tiles, KV_GROUP * HEAD_DIM), lambda b, g, i: (b * nq + i, g)),
        out_shape=jax.ShapeDtypeStruct((t, ATTN_WIDTH), BF16),
        scratch_shapes=[pltpu.SMEM((1,), F32),
                        pltpu.VMEM((n_units, 1, ATTN_TQ), F32),
                        pltpu.VMEM((n_units, 1, ATTN_TQ), F32),
                        pltpu.VMEM((n_units, HEAD_DIM, ATTN_TQ), F32),
                        pltpu.VMEM((2, ATTN_TK, ATTN_TQ), F32),
                        pltpu.VMEM((2, ATTN_TK, ATTN_TQ), BF16),
                        pltpu.VMEM((2, 1, ATTN_TQ), F32)],
        compiler_params=_cparams(("parallel", "parallel", "arbitrary")),
        name="attention",
    )(qt, k, vt)


def _premix_kernel(attn_ref, gb_ref, u_ref, up_ref, un_ref, cw_ref, ga_ref, gc_ref, o_ref, *, n_seq_tiles):
    i = pl.program_id(0)
    tm = u_ref.shape[0]
    o_ref[:, :ATTN_WIDTH] = (_rms(attn_ref[...].astype(F32)) * ga_ref[...]).astype(o_ref.dtype)

    u = u_ref[...].astype(F32)
    first = (i % n_seq_tiles) == 0
    last = (i % n_seq_tiles) == n_seq_tiles - 1
    prev_row = jnp.where(first, 0.0, up_ref[BF16_SUBLANES - 1:BF16_SUBLANES, :].astype(F32))
    next_row = jnp.where(last, 0.0, un_ref[0:1, :].astype(F32))
    row = lax.broadcasted_iota(jnp.int32, u.shape, 0)
    u_prev = jnp.where(row == 0, prev_row, pltpu.roll(u, 1, axis=0))
    u_next = jnp.where(row == tm - 1, next_row, pltpu.roll(u, tm - 1, axis=0))
    cw = cw_ref[...]
    conv = gb_ref[...].astype(F32) * (cw[0:1] * u_prev + cw[1:2] * u + cw[2:3] * u_next)
    o_ref[:, ATTN_WIDTH:] = (_rms(conv) * gc_ref[...]).astype(o_ref.dtype)


def premix(attn, gate_b, u, conv_w, g_attn, g_conv, seq_len, tm=ROW_TILE):
    t = attn.shape[0]
    tm = min(tm, seq_len)
    n_seq_tiles = seq_len // tm
    halo = tm // BF16_SUBLANES
    n_halo = t // BF16_SUBLANES
    row = lambda w: pl.BlockSpec((tm, w), lambda i: (i, 0))
    vec = lambda w: pl.BlockSpec((1, w), lambda i: (0, 0))
    return pl.pallas_call(
        functools.partial(_premix_kernel, n_seq_tiles=n_seq_tiles),
        grid=(t // tm,),
        in_specs=[row(ATTN_WIDTH), row(CONV_WIDTH), row(CONV_WIDTH),
                  pl.BlockSpec((BF16_SUBLANES, CONV_WIDTH), lambda i: (jnp.maximum(i * halo - 1, 0), 0)),
                  pl.BlockSpec((BF16_SUBLANES, CONV_WIDTH),
                               lambda i: (jnp.minimum((i + 1) * halo, n_halo - 1), 0)),
                  pl.BlockSpec((3, CONV_WIDTH), lambda i: (0, 0)),
                  vec(ATTN_WIDTH), vec(CONV_WIDTH)],
        out_specs=row(ATTN_WIDTH + CONV_WIDTH),
        out_shape=jax.ShapeDtypeStruct((t, ATTN_WIDTH + CONV_WIDTH), BF16),
        compiler_params=_cparams(("parallel",)),
        name="premix",
    )(attn, gate_b, u, u, u, conv_w, g_attn.reshape(1, -1), g_conv.reshape(1, -1))


def _rope_tables(seq_len):
    rows = seq_len // GRID_W
    inv = 1.0 / (ROPE_THETA ** (jnp.arange(0, ROPE_AXIS_DIM, 2, dtype=F32) / ROPE_AXIS_DIM))
    row_ang = jnp.arange(rows, dtype=F32)[:, None] * inv
    col_ang = jnp.arange(GRID_W, dtype=F32)[:, None] * inv
    ang = jnp.concatenate([
        jnp.broadcast_to(row_ang[:, None, :], (rows, GRID_W, inv.shape[0])),
        jnp.broadcast_to(col_ang[None, :, :], (rows, GRID_W, inv.shape[0])),
    ], axis=-1).reshape(seq_len, ROPE_AXIS_DIM)
    cos, sin = jnp.cos(ang), jnp.sin(ang)
    return jnp.concatenate([cos, cos], axis=-1), jnp.concatenate([-sin, sin], axis=-1)


def _prep_ffn(w_gate, w_up, w_down):
    return w_gate, w_up, w_down.astype(BF16)


def _ffn(hn, x, w, g_post, g_next):
    wg, wu, wd = w
    h = gate_up(hn, wg, wu)
    y = matmul(h, wd, *DOWN_TILE)
    return resid_norm_rows(y, x, g_post, FFN_RES_SCALE, g_next)


def _trunk(x3d, p):
    b, s, d = x3d.shape
    x = x3d.reshape(b * s, d)
    cos_full, sin_signed = _rope_tables(s)
    hn = rms_norm_rows(x, p["ffn1_pre"])
    x, hn = _ffn(hn, x, p["ffn1"], p["ffn1_post"], p["mix_pre"])
    qt, k, vt = inproj_qkv(hn, p["w_in"], p["q_norm"], p["k_norm"], cos_full, sin_signed, s)
    gate_b, u = inproj_conv(hn, p["w_in"])
    attn = attention(qt, k, vt, b, s)
    mix_in = premix(attn, gate_b, u, p["conv_w"], p["attn_out_norm"], p["conv_out_norm"], s)
    y = matmul(mix_in, p["w_out"], *OUT_PROJ_TILE)
    x, hn = resid_norm_rows(y, x, p["mix_post"], 1.0, p["ffn2_pre"])
    x, _ = _ffn(hn, x, p["ffn2"], p["ffn2_post"], None)
    return x.reshape(b, s, d)


def kernel(x_prompt, x_sample, ffn1_pre, ffn1_post, w1_gate, w1_up, w1_down, mix_pre, mix_post, w_in, q_norm, k_norm, conv_w, attn_out_norm, conv_out_norm, w_out, ffn2_pre, ffn2_post, w2_gate, w2_up, w2_down):
    p = {
        "ffn1_pre": ffn1_pre[0], "ffn1_post": ffn1_post[0],
        "ffn1": _prep_ffn(w1_gate[0], w1_up[0], w1_down[0]),
        "mix_pre": mix_pre[0], "mix_post": mix_post[0],
        "w_in": w_in[0].astype(BF16), "q_norm": q_norm[0], "k_norm": k_norm[0],
        "conv_w": conv_w[0], "attn_out_norm": attn_out_norm[0], "conv_out_norm": conv_out_norm[0],
        "w_out": w_out[0].astype(BF16),
        "ffn2_pre": ffn2_pre[0], "ffn2_post": ffn2_post[0],
        "ffn2": _prep_ffn(w2_gate[0], w2_up[0], w2_down[0]),
    }
    return _trunk(x_prompt, p), _trunk(x_sample, p)
```

```python
import functools

import jax
import jax.numpy as jnp
from jax import lax
from jax.experimental import pallas as pl
from jax.experimental.pallas import tpu as pltpu

F32 = jnp.float32
BF16 = jnp.bfloat16

NORM_EPS = 1e-6
FFN_RES_SCALE = 0.5
HEAD_DIM = 128
N_Q_HEADS = 16
N_KV_HEADS = 4
KV_GROUP = N_Q_HEADS // N_KV_HEADS
ATTN_WIDTH = N_Q_HEADS * HEAD_DIM
KV_WIDTH = N_KV_HEADS * HEAD_DIM
CONV_WIDTH = 2048
GRID_W = 64
ROPE_THETA = 10000.0
ROPE_AXIS_DIM = HEAD_DIM // 2

BF16_SUBLANES = 16
VMEM_LIMIT_BYTES = 56 * 1024 * 1024
GATE_UP_TILE = (4096, 256)
GATE_UP_SUB_ROWS = 1024
DOWN_TILE = (512, 512)
OUT_PROJ_TILE = (1024, 1024)
IN_PROJ_TM = 1024
ROW_TILE = 256


def _cparams(sem):
    return pltpu.CompilerParams(dimension_semantics=sem, vmem_limit_bytes=VMEM_LIMIT_BYTES)


def _rms(x):
    return x * lax.rsqrt(jnp.mean(x * x, axis=-1, keepdims=True) + NORM_EPS)


def _norm_kernel(x_ref, g_ref, o_ref):
    o_ref[...] = (_rms(x_ref[...]) * g_ref[...]).astype(o_ref.dtype)


def rms_norm_rows(x, g, tm=ROW_TILE):
    t, d = x.shape
    tm = min(tm, t)
    return pl.pallas_call(
        _norm_kernel,
        grid=(t // tm,),
        in_specs=[pl.BlockSpec((tm, d), lambda i: (i, 0)),
                  pl.BlockSpec((1, d), lambda i: (0, 0))],
        out_specs=pl.BlockSpec((tm, d), lambda i: (i, 0)),
        out_shape=jax.ShapeDtypeStruct((t, d), BF16),
        compiler_params=_cparams(("parallel",)),
        name="rms_norm_rows",
    )(x, g.reshape(1, d))


def _resid_norm_kernel(y_ref, x_ref, gp_ref, gn_ref, xo_ref, ho_ref, *, scale):
    x_new = x_ref[...] + scale * (_rms(y_ref[...]) * gp_ref[...])
    xo_ref[...] = x_new
    ho_ref[...] = (_rms(x_new) * gn_ref[...]).astype(ho_ref.dtype)


def _resid_kernel(y_ref, x_ref, gp_ref, xo_ref, *, scale):
    xo_ref[...] = x_ref[...] + scale * (_rms(y_ref[...]) * gp_ref[...])


def resid_norm_rows(y, x, g_post, scale, g_next=None, tm=ROW_TILE):
    t, d = x.shape
    tm = min(tm, t)
    row = pl.BlockSpec((tm, d), lambda i: (i, 0))
    vec = pl.BlockSpec((1, d), lambda i: (0, 0))
    if g_next is None:
        return pl.pallas_call(
            functools.partial(_resid_kernel, scale=scale),
            grid=(t // tm,),
            in_specs=[row, row, vec],
            out_specs=row,
            out_shape=jax.ShapeDtypeStruct((t, d), F32),
            compiler_params=_cparams(("parallel",)),
            name="resid_rows",
        )(y, x, g_post.reshape(1, d)), None
    return pl.pallas_call(
        functools.partial(_resid_norm_kernel, scale=scale),
        grid=(t // tm,),
        in_specs=[row, row, vec, vec],
        out_specs=[row, row],
        out_shape=[jax.ShapeDtypeStruct((t, d), F32), jax.ShapeDtypeStruct((t, d), BF16)],
        compiler_params=_cparams(("parallel",)),
        name="resid_norm_rows",
    )(y, x, g_post.reshape(1, d), g_next.reshape(1, d))


def _gateup_kernel(a_ref, wg_ref, wu_ref, o_ref):
    sub = min(GATE_UP_SUB_ROWS, a_ref.shape[0])
    wg = wg_ref[...].astype(a_ref.dtype)
    wu = wu_ref[...].astype(a_ref.dtype)
    for r in range(0, a_ref.shape[0], sub):
        a = a_ref[r:r + sub, :]
        g = jnp.dot(a, wg, preferred_element_type=F32)
        u = jnp.dot(a, wu, preferred_element_type=F32)
        o_ref[r:r + sub, :] = (g * jax.nn.sigmoid(g) * u).astype(o_ref.dtype)


def gate_up(a, wg, wu):
    t, d = a.shape
    f = wg.shape[1]
    tm, tn = min(GATE_UP_TILE[0], t), GATE_UP_TILE[1]
    return pl.pallas_call(
        _gateup_kernel,
        grid=(t // tm, f // tn),
        in_specs=[pl.BlockSpec((tm, d), lambda i, j: (i, 0), pipeline_mode=pl.Buffered(1)),
                  pl.BlockSpec((d, tn), lambda i, j: (0, j)),
                  pl.BlockSpec((d, tn), lambda i, j: (0, j))],
        out_specs=pl.BlockSpec((tm, tn), lambda i, j: (i, j)),
        out_shape=jax.ShapeDtypeStruct((t, f), BF16),
        compiler_params=_cparams(("parallel", "arbitrary")),
        name="gate_up",
    )(a, wg, wu)


def _mm_kernel(a_ref, w_ref, o_ref):
    o_ref[...] = jnp.dot(a_ref[...], w_ref[...], preferred_element_type=F32)


def column_tiles(w, tn):
    kd, n = w.shape
    return w.astype(BF16).reshape(kd, n // tn, tn).transpose(1, 0, 2)


def matmul(a, w_tiles, tm):
    t, kd = a.shape
    n_tiles, _, tn = w_tiles.shape
    tm = min(tm, t)
    return pl.pallas_call(
        _mm_kernel,
        grid=(t // tm, n_tiles),
        in_specs=[pl.BlockSpec((tm, kd), lambda i, j: (i, 0)),
                  pl.BlockSpec((None, kd, tn), lambda i, j: (j, 0, 0))],
        out_specs=pl.BlockSpec((tm, tn), lambda i, j: (i, j)),
        out_shape=jax.ShapeDtypeStruct((t, n_tiles * tn), F32),
        compiler_params=_cparams(("parallel", "arbitrary")),
        name="matmul_fullk",
    )(a, w_tiles)


QKV_TN = KV_GROUP * HEAD_DIM
N_Q_TILES = ATTN_WIDTH // QKV_TN
ATTN_TQ = 256
ATTN_TK = 1024
ATTN_SLAB = 64
ATTN_Q_TILES = 4
ATTN_CHUNK_UNROLL = 4
SCORE_BOUND_LOG2 = 40.0
Q_SCALE = (HEAD_DIM ** -0.5) * 1.4426950408889634


def _inproj_qkv_kernel(a_ref, w_ref, qg_ref, kg_ref, cos_ref, sin_ref, qt_ref, k_ref, vt_ref, raw_ref):
    j = pl.program_id(1)
    tm = a_ref.shape[0]
    n_steps = N_Q_TILES + 3

    def matmul(slot):
        raw_ref[slot] = jnp.dot(a_ref[...], w_ref[...], preferred_element_type=F32)

    def norm_rope(y, gain):
        y = _rms(y) * gain
        return y * cos_ref[...] + pltpu.roll(y, ROPE_AXIS_DIM, axis=1) * sin_ref[...]

    def store_transposed(dst_ref, h, y, width):
        yt = y.T.astype(dst_ref.dtype)
        for c in range(tm // width):
            dst_ref[h, c] = yt[:, c * width:(c + 1) * width]

    def epilogue(tile, slot):
        for h in range(KV_GROUP):
            y = raw_ref[slot, :, h * HEAD_DIM:(h + 1) * HEAD_DIM]
            if tile < N_Q_TILES:
                store_transposed(qt_ref, h, norm_rope(y, qg_ref[...]) * Q_SCALE, ATTN_TQ)
            elif tile == N_Q_TILES:
                k_ref[h] = norm_rope(y, kg_ref[...]).astype(k_ref.dtype)
            else:
                store_transposed(vt_ref, h, y, ATTN_TK)

    for step in range(n_steps):
        @pl.when(j == step)
        def _(step=step):
            if step < n_steps - 1:
                matmul(step % 2)
            if step > 0:
                epilogue(step - 1, (step - 1) % 2)


def inproj_qkv(a, w_in, q_norm, k_norm, cos_full, sin_signed, seq_len, tm=IN_PROJ_TM):
    t, d = a.shape
    tm = min(tm, seq_len)
    n_seq_tiles = seq_len // tm
    return pl.pallas_call(
        _inproj_qkv_kernel,
        grid=(t // tm, N_Q_TILES + 3),
        in_specs=[pl.BlockSpec((tm, d), lambda i, j: (i, 0)),
                  pl.BlockSpec((d, QKV_TN), lambda i, j: (0, jnp.minimum(j, N_Q_TILES + 1))),
                  pl.BlockSpec((1, HEAD_DIM), lambda i, j: (0, 0)),
                  pl.BlockSpec((1, HEAD_DIM), lambda i, j: (0, 0)),
                  pl.BlockSpec((tm, HEAD_DIM), lambda i, j: (i % n_seq_tiles, 0)),
                  pl.BlockSpec((tm, HEAD_DIM), lambda i, j: (i % n_seq_tiles, 0))],
        out_specs=[pl.BlockSpec((KV_GROUP, tm // ATTN_TQ, HEAD_DIM, ATTN_TQ),
                                lambda i, j: (jnp.clip(j - 1, 0, N_Q_TILES - 1), i, 0, 0)),
                   pl.BlockSpec((N_KV_HEADS, tm, HEAD_DIM), lambda i, j: (0, i, 0)),
                   pl.BlockSpec((N_KV_HEADS, tm // ATTN_TK, HEAD_DIM, ATTN_TK), lambda i, j: (0, i, 0, 0))],
        out_shape=[jax.ShapeDtypeStruct((N_Q_HEADS, t // ATTN_TQ, HEAD_DIM, ATTN_TQ), BF16),
                   jax.ShapeDtypeStruct((N_KV_HEADS, t, HEAD_DIM), BF16),
                   jax.ShapeDtypeStruct((N_KV_HEADS, t // ATTN_TK, HEAD_DIM, ATTN_TK), BF16)],
        scratch_shapes=[pltpu.VMEM((2, tm, QKV_TN), F32)],
        compiler_params=_cparams(("parallel", "arbitrary")),
        name="inproj_qkv",
    )(a, w_in, q_norm.reshape(1, HEAD_DIM), k_norm.reshape(1, HEAD_DIM), cos_full, sin_signed)


def _inproj_conv_kernel(a_ref, wb_ref, wc_ref, wh_ref, gb_ref, u_ref):
    a = a_ref[...]
    gb_ref[...] = jnp.dot(a, wb_ref[...], preferred_element_type=F32).astype(gb_ref.dtype)
    c = jnp.dot(a, wc_ref[...], preferred_element_type=F32)
    h = jnp.dot(a, wh_ref[...], preferred_element_type=F32)
    u_ref[...] = (c * h).astype(u_ref.dtype)


def inproj_conv(a, w_in, tm=IN_PROJ_TM, tn=QKV_TN):
    t, d = a.shape
    tm = min(tm, t)
    off_b = (ATTN_WIDTH + 2 * KV_WIDTH) // tn
    off_c = off_b + CONV_WIDTH // tn
    off_h = off_c + CONV_WIDTH // tn
    out = jax.ShapeDtypeStruct((t, CONV_WIDTH), BF16)
    return pl.pallas_call(
        _inproj_conv_kernel,
        grid=(t // tm, CONV_WIDTH // tn),
        in_specs=[pl.BlockSpec((tm, d), lambda i, j: (i, 0)),
                  pl.BlockSpec((d, tn), lambda i, j: (0, off_b + j)),
                  pl.BlockSpec((d, tn), lambda i, j: (0, off_c + j)),
                  pl.BlockSpec((d, tn), lambda i, j: (0, off_h + j))],
        out_specs=[pl.BlockSpec((tm, tn), lambda i, j: (i, j)),
                   pl.BlockSpec((tm, tn), lambda i, j: (i, j))],
        out_shape=[out, out],
        compiler_params=_cparams(("parallel", "arbitrary")),
        name="inproj_conv",
    )(a, w_in, w_in, w_in)


def _attn_kernel(qt_ref, k_ref, vt_ref, o_ref, ksq_ref, m_ref, l_ref, acc_ref, s_ref, p_ref, alpha_ref):
    group, n_q_tiles, _, tq = qt_ref.shape
    n_heads = group * n_q_tiles
    n_chunks = vt_ref.shape[1]
    tk = vt_ref.shape[3]
    slabs = [pl.ds(r, ATTN_SLAB) for r in range(0, tk, ATTN_SLAB)]

    def q_tile(h):
        return qt_ref[h % group, h // group]

    def key_chunk(c):
        return k_ref[0, pl.ds(pl.multiple_of(c * tk, tk), tk), :]

    @pl.when(pl.program_id(2) == 0)
    def _():
        def chunk_max(c, best):
            kc = key_chunk(c).astype(F32)
            return jnp.maximum(best, jnp.max(jnp.sum(kc * kc, axis=1, keepdims=True)))
        ksq_ref[0] = lax.fori_loop(0, n_chunks, chunk_max, jnp.float32(0.0))

    qsq = jnp.float32(0.0)
    for h in range(n_heads):
        q = q_tile(h).astype(F32)
        qsq = jnp.maximum(qsq, jnp.max(jnp.sum(q * q, axis=0, keepdims=True)))
    scores_bounded = qsq * ksq_ref[0] <= SCORE_BOUND_LOG2 * SCORE_BOUND_LOG2

    l_ref[...] = jnp.zeros(l_ref.shape, F32)
    acc_ref[...] = jnp.zeros(acc_ref.shape, F32)
    p_ref[...] = jnp.zeros(p_ref.shape, p_ref.dtype)

    def pipeline(stage_scores, stage_values, unroll):
        unroll = unroll if n_chunks % unroll == 0 else 1

        @pl.loop(0, n_chunks // unroll)
        def _(step):
            for sub in range(unroll):
                c = step * unroll + sub
                for h in range(n_heads):
                    slot, other = h % 2, (h + 1) % 2
                    stage_scores(c, h, slot)
                    if h > 0:
                        stage_values(c, h - 1, other)
                    else:
                        stage_values(jnp.maximum(c - 1, 0), n_heads - 1, other)
        stage_values(n_chunks - 1, n_heads - 1, (n_heads - 1) % 2)

    def unshifted():
        def probs(c, h, slot):
            s = jnp.dot(key_chunk(c), q_tile(h), preferred_element_type=F32)
            slab_sum = jnp.zeros((ATTN_SLAB, s.shape[1]), F32)
            for r in range(0, tk, ATTN_SLAB):
                p = jnp.exp2(s[r:r + ATTN_SLAB])
                slab_sum = slab_sum + p
                p_ref[slot, pl.ds(r, ATTN_SLAB), :] = p.astype(p_ref.dtype)
            l_ref[h] = l_ref[h] + jnp.sum(slab_sum, axis=0, keepdims=True)

        def values(c, h, slot):
            acc_ref[h] = acc_ref[h] + jnp.dot(vt_ref[0, c], p_ref[slot], preferred_element_type=F32)

        pipeline(probs, values, ATTN_CHUNK_UNROLL)

    def running_max():
        def scores(c, h, slot):
            s_ref[slot] = jnp.dot(key_chunk(c), q_tile(h), preferred_element_type=F32)

        def softmax(h, slot):
            m_old = m_ref[h]
            slab_max = s_ref[slot, slabs[0], :]
            for rows in slabs[1:]:
                slab_max = jnp.maximum(slab_max, s_ref[slot, rows, :])
            m_new = jnp.maximum(m_old, jnp.max(slab_max, axis=0, keepdims=True))
            alpha = jnp.exp2(m_old - m_new)
            slab_sum = jnp.zeros(slab_max.shape, F32)
            for rows in slabs:
                p = jnp.exp2(s_ref[slot, rows, :] - m_new)
                slab_sum = slab_sum + p
                p_ref[slot, rows, :] = p.astype(p_ref.dtype)
            l_ref[h] = alpha * l_ref[h] + jnp.sum(slab_sum, axis=0, keepdims=True)
            m_ref[h] = m_new
            alpha_ref[slot] = alpha

        def scores_and_softmax(c, h, slot):
            if h + 1 < n_heads:
                scores(c, h + 1, (h + 1) % 2)
            else:
                scores(jnp.minimum(c + 1, n_chunks - 1), 0, (h + 1) % 2)
            softmax(h, slot)

        def values(c, h, slot):
            pv = jnp.dot(vt_ref[0, c], p_ref[slot], preferred_element_type=F32)
            acc_ref[h] = alpha_ref[slot] * acc_ref[h] + pv

        m_ref[...] = jnp.full(m_ref.shape, -jnp.inf, F32)
        alpha_ref[...] = jnp.ones(alpha_ref.shape, F32)
        scores(0, 0, 0)
        pipeline(scores_and_softmax, values, 1)

    lax.cond(scores_bounded, unshifted, running_max)

    for h in range(n_heads):
        out_t = acc_ref[h] / l_ref[h]
        head, tile = h % group, h // group
        o_ref[tile * tq:(tile + 1) * tq, head * HEAD_DIM:(head + 1) * HEAD_DIM] = out_t.T.astype(o_ref.dtype)


def attention(qt, k, vt, batch, seq_len):
    t = k.shape[1]
    q_tiles = min(ATTN_Q_TILES, seq_len // ATTN_TQ)
    nq = seq_len // (ATTN_TQ * q_tiles)
    nk = seq_len // ATTN_TK
    n_units = KV_GROUP * q_tiles
    return pl.pallas_call(
        _attn_kernel,
        grid=(batch, N_KV_HEADS, nq),
        in_specs=[pl.BlockSpec((KV_GROUP, q_tiles, HEAD_DIM, ATTN_TQ), lambda b, g, i: (g, b * nq + i, 0, 0)),
                  pl.BlockSpec((1, seq_len, HEAD_DIM), lambda b, g, i: (g, b, 0)),
                  pl.BlockSpec((1, nk, HEAD_DIM, ATTN_TK), lambda b, g, i: (g, b, 0, 0))],
        out_specs=pl.BlockSpec((ATTN_TQ * q_tiles, KV_GROUP * HEAD_DIM), lambda b, g, i: (b * nq + i, g)),
        out_shape=jax.ShapeDtypeStruct((t, ATTN_WIDTH), BF16),
        scratch_shapes=[pltpu.SMEM((1,), F32),
                        pltpu.VMEM((n_units, 1, ATTN_TQ), F32),
                        pltpu.VMEM((n_units, 1, ATTN_TQ), F32),
                        pltpu.VMEM((n_units, HEAD_DIM, ATTN_TQ), F32),
                        pltpu.VMEM((2, ATTN_TK, ATTN_TQ), F32),
                        pltpu.VMEM((2, ATTN_TK, ATTN_TQ), BF16),
                        pltpu.VMEM((2, 1, ATTN_TQ), F32)],
        compiler_params=_cparams(("parallel", "parallel", "arbitrary")),
        name="attention",
    )(qt, k, vt)


def _premix_kernel(attn_ref, gb_ref, u_ref, up_ref, un_ref, cw_ref, ga_ref, gc_ref, o_ref, *, n_seq_tiles):
    i = pl.program_id(0)
    tm = u_ref.shape[0]
    o_ref[:, :ATTN_WIDTH] = (_rms(attn_ref[...].astype(F32)) * ga_ref[...]).astype(o_ref.dtype)

    u = u_ref[...].astype(F32)
    first = (i % n_seq_tiles) == 0
    last = (i % n_seq_tiles) == n_seq_tiles - 1
    prev_row = jnp.where(first, 0.0, up_ref[BF16_SUBLANES - 1:BF16_SUBLANES, :].astype(F32))
    next_row = jnp.where(last, 0.0, un_ref[0:1, :].astype(F32))
    row = lax.broadcasted_iota(jnp.int32, u.shape, 0)
    u_prev = jnp.where(row == 0, prev_row, pltpu.roll(u, 1, axis=0))
    u_next = jnp.where(row == tm - 1, next_row, pltpu.roll(u, tm - 1, axis=0))
    cw = cw_ref[...]
    conv = gb_ref[...].astype(F32) * (cw[0:1] * u_prev + cw[1:2] * u + cw[2:3] * u_next)
    o_ref[:, ATTN_WIDTH:] = (_rms(conv) * gc_ref[...]).astype(o_ref.dtype)


def premix(attn, gate_b, u, conv_w, g_attn, g_conv, seq_len, tm=ROW_TILE):
    t = attn.shape[0]
    tm = min(tm, seq_len)
    n_seq_tiles = seq_len // tm
    halo = tm // BF16_SUBLANES
    n_halo = t // BF16_SUBLANES
    row = lambda w: pl.BlockSpec((tm, w), lambda i: (i, 0))
    vec = lambda w: pl.BlockSpec((1, w), lambda i: (0, 0))
    return pl.pallas_call(
        functools.partial(_premix_kernel, n_seq_tiles=n_seq_tiles),
        grid=(t // tm,),
        in_specs=[row(ATTN_WIDTH), row(CONV_WIDTH), row(CONV_WIDTH),
                  pl.BlockSpec((BF16_SUBLANES, CONV_WIDTH), lambda i: (jnp.maximum(i * halo - 1, 0), 0)),
                  pl.BlockSpec((BF16_SUBLANES, CONV_WIDTH),
                               lambda i: (jnp.minimum((i + 1) * halo, n_halo - 1), 0)),
                  pl.BlockSpec((3, CONV_WIDTH), lambda i: (0, 0)),
                  vec(ATTN_WIDTH), vec(CONV_WIDTH)],
        out_specs=row(ATTN_WIDTH + CONV_WIDTH),
        out_shape=jax.ShapeDtypeStruct((t, ATTN_WIDTH + CONV_WIDTH), BF16),
        compiler_params=_cparams(("parallel",)),
        name="premix",
    )(attn, gate_b, u, u, u, conv_w, g_attn.reshape(1, -1), g_conv.reshape(1, -1))


def _rope_tables(seq_len):
    rows = seq_len // GRID_W
    inv = 1.0 / (ROPE_THETA ** (jnp.arange(0, ROPE_AXIS_DIM, 2, dtype=F32) / ROPE_AXIS_DIM))
    row_ang = jnp.arange(rows, dtype=F32)[:, None] * inv
    col_ang = jnp.arange(GRID_W, dtype=F32)[:, None] * inv
    ang = jnp.concatenate([
        jnp.broadcast_to(row_ang[:, None, :], (rows, GRID_W, inv.shape[0])),
        jnp.broadcast_to(col_ang[None, :, :], (rows, GRID_W, inv.shape[0])),
    ], axis=-1).reshape(seq_len, ROPE_AXIS_DIM)
    cos, sin = jnp.cos(ang), jnp.sin(ang)
    return jnp.concatenate([cos, cos], axis=-1), jnp.concatenate([-sin, sin], axis=-1)


def _prep_ffn(w_gate, w_up, w_down):
    return w_gate, w_up, column_tiles(w_down, DOWN_TILE[1])


def _ffn(hn, x, w, g_post, g_next):
    wg, wu, wd = w
    h = gate_up(hn, wg, wu)
    y = matmul(h, wd, DOWN_TILE[0])
    return resid_norm_rows(y, x, g_post, FFN_RES_SCALE, g_next)


def _trunk(x3d, p):
    b, s, d = x3d.shape
    x = x3d.reshape(b * s, d)
    cos_full, sin_signed = _rope_tables(s)
    hn = rms_norm_rows(x, p["ffn1_pre"])
    x, hn = _ffn(hn, x, p["ffn1"], p["ffn1_post"], p["mix_pre"])
    qt, k, vt = inproj_qkv(hn, p["w_in"], p["q_norm"], p["k_norm"], cos_full, sin_signed, s)
    gate_b, u = inproj_conv(hn, p["w_in"])
    attn = attention(qt, k, vt, b, s)
    mix_in = premix(attn, gate_b, u, p["conv_w"], p["attn_out_norm"], p["conv_out_norm"], s)
    y = matmul(mix_in, p["w_out"], OUT_PROJ_TILE[0])
    x, hn = resid_norm_rows(y, x, p["mix_post"], 1.0, p["ffn2_pre"])
    x, _ = _ffn(hn, x, p["ffn2"], p["ffn2_post"], None)
    return x.reshape(b, s, d)


def kernel(x_prompt, x_sample, ffn1_pre, ffn1_post, w1_gate, w1_up, w1_down, mix_pre, mix_post, w_in, q_norm, k_norm, conv_w, attn_out_norm, conv_out_norm, w_out, ffn2_pre, ffn2_post, w2_gate, w2_up, w2_down):
    p = {
        "ffn1_pre": ffn1_pre[0], "ffn1_post": ffn1_post[0],
        "ffn1": _prep_ffn(w1_gate[0], w1_up[0], w1_down[0]),
        "mix_pre": mix_pre[0], "mix_post": mix_post[0],
        "w_in": w_in[0].astype(BF16), "q_norm": q_norm[0], "k_norm": k_norm[0],
        "conv_w": conv_w[0], "attn_out_norm": attn_out_norm[0], "conv_out_norm": conv_out_norm[0],
        "w_out": column_tiles(w_out[0], OUT_PROJ_TILE[1]),
        "ffn2_pre": ffn2_pre[0], "ffn2_post": ffn2_post[0],
        "ffn2": _prep_ffn(w2_gate[0], w2_up[0], w2_down[0]),
    }
    return _trunk(x_prompt, p), _trunk(x_sample, p)
```

```python
import functools

import jax
import jax.numpy as jnp
from jax import lax
from jax.experimental import pallas as pl
from jax.experimental.pallas import tpu as pltpu

F32 = jnp.float32
BF16 = jnp.bfloat16

NORM_EPS = 1e-6
FFN_RES_SCALE = 0.5
HEAD_DIM = 128
N_Q_HEADS = 16
N_KV_HEADS = 4
KV_GROUP = N_Q_HEADS // N_KV_HEADS
ATTN_WIDTH = N_Q_HEADS * HEAD_DIM
KV_WIDTH = N_KV_HEADS * HEAD_DIM
CONV_WIDTH = 2048
GRID_W = 64
ROPE_THETA = 10000.0
ROPE_AXIS_DIM = HEAD_DIM // 2

BF16_SUBLANES = 16
VMEM_LIMIT_BYTES = 56 * 1024 * 1024
GATE_UP_TILE = (4096, 256)
GATE_UP_SUB_ROWS = 1024
DOWN_TILE = (512, 512)
OUT_PROJ_TILE = (512, 512)
EPILOGUE_ROWS = 64
IN_PROJ_TM = 1024
ROW_TILE = 256


def _cparams(sem):
    return pltpu.CompilerParams(dimension_semantics=sem, vmem_limit_bytes=VMEM_LIMIT_BYTES)


def _rms(x):
    return x * lax.rsqrt(jnp.mean(x * x, axis=-1, keepdims=True) + NORM_EPS)


def _norm_kernel(x_ref, g_ref, o_ref):
    o_ref[...] = (_rms(x_ref[...]) * g_ref[...]).astype(o_ref.dtype)


def rms_norm_rows(x, g, tm=ROW_TILE):
    t, d = x.shape
    tm = min(tm, t)
    return pl.pallas_call(
        _norm_kernel,
        grid=(t // tm,),
        in_specs=[pl.BlockSpec((tm, d), lambda i: (i, 0)),
                  pl.BlockSpec((1, d), lambda i: (0, 0))],
        out_specs=pl.BlockSpec((tm, d), lambda i: (i, 0)),
        out_shape=jax.ShapeDtypeStruct((t, d), BF16),
        compiler_params=_cparams(("parallel",)),
        name="rms_norm_rows",
    )(x, g.reshape(1, d))


def _resid_norm_kernel(y_ref, x_ref, gp_ref, gn_ref, xo_ref, ho_ref, *, scale):
    x_new = x_ref[...] + scale * (_rms(y_ref[...]) * gp_ref[...])
    xo_ref[...] = x_new
    ho_ref[...] = (_rms(x_new) * gn_ref[...]).astype(ho_ref.dtype)


def _resid_kernel(y_ref, x_ref, gp_ref, xo_ref, *, scale):
    xo_ref[...] = x_ref[...] + scale * (_rms(y_ref[...]) * gp_ref[...])


def resid_norm_rows(y, x, g_post, scale, g_next=None, tm=ROW_TILE):
    t, d = x.shape
    tm = min(tm, t)
    row = pl.BlockSpec((tm, d), lambda i: (i, 0))
    vec = pl.BlockSpec((1, d), lambda i: (0, 0))
    if g_next is None:
        return pl.pallas_call(
            functools.partial(_resid_kernel, scale=scale),
            grid=(t // tm,),
            in_specs=[row, row, vec],
            out_specs=row,
            out_shape=jax.ShapeDtypeStruct((t, d), F32),
            compiler_params=_cparams(("parallel",)),
            name="resid_rows",
        )(y, x, g_post.reshape(1, d)), None
    return pl.pallas_call(
        functools.partial(_resid_norm_kernel, scale=scale),
        grid=(t // tm,),
        in_specs=[row, row, vec, vec],
        out_specs=[row, row],
        out_shape=[jax.ShapeDtypeStruct((t, d), F32), jax.ShapeDtypeStruct((t, d), BF16)],
        compiler_params=_cparams(("parallel",)),
        name="resid_norm_rows",
    )(y, x, g_post.reshape(1, d), g_next.reshape(1, d))


def _gateup_kernel(a_ref, wg_ref, wu_ref, o_ref):
    sub = min(GATE_UP_SUB_ROWS, a_ref.shape[0])
    wg = wg_ref[...].astype(a_ref.dtype)
    wu = wu_ref[...].astype(a_ref.dtype)
    for r in range(0, a_ref.shape[0], sub):
        a = a_ref[r:r + sub, :]
        g = jnp.dot(a, wg, preferred_element_type=F32)
        u = jnp.dot(a, wu, preferred_element_type=F32)
        o_ref[r:r + sub, :] = (g * jax.nn.sigmoid(g) * u).astype(o_ref.dtype)


def gate_up(a, wg, wu):
    t, d = a.shape
    f = wg.shape[1]
    tm, tn = min(GATE_UP_TILE[0], t), GATE_UP_TILE[1]
    return pl.pallas_call(
        _gateup_kernel,
        grid=(t // tm, f // tn),
        in_specs=[pl.BlockSpec((tm, d), lambda i, j: (i, 0), pipeline_mode=pl.Buffered(1)),
                  pl.BlockSpec((d, tn), lambda i, j: (0, j)),
                  pl.BlockSpec((d, tn), lambda i, j: (0, j))],
        out_specs=pl.BlockSpec((tm, tn), lambda i, j: (i, j)),
        out_shape=jax.ShapeDtypeStruct((t, f), BF16),
        compiler_params=_cparams(("parallel", "arbitrary")),
        name="gate_up",
    )(a, wg, wu)


def _mm_kernel(a_ref, w_ref, o_ref):
    o_ref[...] = jnp.dot(a_ref[...], w_ref[...], preferred_element_type=F32)


def matmul(a, w, tm, tn):
    t, kd = a.shape
    n = w.shape[1]
    tm, tn = min(tm, t), min(tn, n)
    return pl.pallas_call(
        _mm_kernel,
        grid=(t // tm, n // tn),
        in_specs=[pl.BlockSpec((tm, kd), lambda i, j: (i, 0)),
                  pl.BlockSpec((kd, tn), lambda i, j: (0, j))],
        out_specs=pl.BlockSpec((tm, tn), lambda i, j: (i, j)),
        out_shape=jax.ShapeDtypeStruct((t, n), F32),
        compiler_params=_cparams(("parallel", "arbitrary")),
        name="matmul_fullk",
    )(a, w)


def _mm_resid_norm_kernel(a_ref, w_ref, x_ref, gp_ref, gn_ref, xo_ref, ho_ref, *, scale, n_col_tiles):
    j = pl.program_id(1)
    tm = a_ref.shape[0]
    tn = w_ref.shape[1]
    for col in range(n_col_tiles):
        @pl.when(j == col)
        def _(col=col):
            xo_ref[:, col * tn:(col + 1) * tn] = jnp.dot(a_ref[...], w_ref[...], preferred_element_type=F32)

    @pl.when(j == n_col_tiles - 1)
    def _():
        for r in range(0, tm, EPILOGUE_ROWS):
            rows = pl.ds(r, EPILOGUE_ROWS)
            x_new = x_ref[rows, :] + scale * (_rms(xo_ref[rows, :]) * gp_ref[...])
            xo_ref[rows, :] = x_new
            ho_ref[rows, :] = (_rms(x_new) * gn_ref[...]).astype(ho_ref.dtype)


def matmul_resid_norm(a, w, x, g_post, scale, g_next, tm, tn):
    t, kd = a.shape
    d = w.shape[1]
    tm = min(tm, t)
    vec = pl.BlockSpec((1, d), lambda i, j: (0, 0))
    return pl.pallas_call(
        functools.partial(_mm_resid_norm_kernel, scale=scale, n_col_tiles=d // tn),
        grid=(t // tm, d // tn),
        in_specs=[pl.BlockSpec((tm, kd), lambda i, j: (i, 0)),
                  pl.BlockSpec((kd, tn), lambda i, j: (0, j)),
                  pl.BlockSpec((tm, d), lambda i, j: (i, 0), pipeline_mode=pl.Buffered(1)),
                  vec, vec],
        out_specs=[pl.BlockSpec((tm, d), lambda i, j: (i, 0)),
                   pl.BlockSpec((tm, d), lambda i, j: (i, 0))],
        out_shape=[jax.ShapeDtypeStruct((t, d), F32), jax.ShapeDtypeStruct((t, d), BF16)],
        compiler_params=_cparams(("parallel", "arbitrary")),
        name="matmul_resid_norm",
    )(a, w, x, g_post.reshape(1, d), g_next.reshape(1, d))


QKV_TN = KV_GROUP * HEAD_DIM
N_Q_TILES = ATTN_WIDTH // QKV_TN
ATTN_TQ = 256
ATTN_TK = 1024
ATTN_SLAB = 64
ATTN_Q_TILES = 4
ATTN_CHUNK_UNROLL = 4
SCORE_BOUND_LOG2 = 40.0
Q_SCALE = (HEAD_DIM ** -0.5) * 1.4426950408889634


def _inproj_qkv_kernel(a_ref, w_ref, qg_ref, kg_ref, cos_ref, sin_ref, qt_ref, k_ref, vt_ref, raw_ref):
    j = pl.program_id(1)
    tm = a_ref.shape[0]
    n_steps = N_Q_TILES + 3

    def matmul(slot):
        raw_ref[slot] = jnp.dot(a_ref[...], w_ref[...], preferred_element_type=F32)

    def norm_rope(y, gain):
        y = _rms(y) * gain
        return y * cos_ref[...] + pltpu.roll(y, ROPE_AXIS_DIM, axis=1) * sin_ref[...]

    def store_transposed(dst_ref, h, y, width):
        yt = y.T.astype(dst_ref.dtype)
        for c in range(tm // width):
            dst_ref[h, c] = yt[:, c * width:(c + 1) * width]

    def epilogue(tile, slot):
        for h in range(KV_GROUP):
            y = raw_ref[slot, :, h * HEAD_DIM:(h + 1) * HEAD_DIM]
            if tile < N_Q_TILES:
                store_transposed(qt_ref, h, norm_rope(y, qg_ref[...]) * Q_SCALE, ATTN_TQ)
            elif tile == N_Q_TILES:
                k_ref[h] = norm_rope(y, kg_ref[...]).astype(k_ref.dtype)
            else:
                store_transposed(vt_ref, h, y, ATTN_TK)

    for step in range(n_steps):
        @pl.when(j == step)
        def _(step=step):
            if step < n_steps - 1:
                matmul(step % 2)
            if step > 0:
                epilogue(step - 1, (step - 1) % 2)


def inproj_qkv(a, w_in, q_norm, k_norm, cos_full, sin_signed, seq_len, tm=IN_PROJ_TM):
    t, d = a.shape
    tm = min(tm, seq_len)
    n_seq_tiles = seq_len // tm
    return pl.pallas_call(
        _inproj_qkv_kernel,
        grid=(t // tm, N_Q_TILES + 3),
        in_specs=[pl.BlockSpec((tm, d), lambda i, j: (i, 0)),
                  pl.BlockSpec((d, QKV_TN), lambda i, j: (0, jnp.minimum(j, N_Q_TILES + 1))),
                  pl.BlockSpec((1, HEAD_DIM), lambda i, j: (0, 0)),
                  pl.BlockSpec((1, HEAD_DIM), lambda i, j: (0, 0)),
                  pl.BlockSpec((tm, HEAD_DIM), lambda i, j: (i % n_seq_tiles, 0)),
                  pl.BlockSpec((tm, HEAD_DIM), lambda i, j: (i % n_seq_tiles, 0))],
        out_specs=[pl.BlockSpec((KV_GROUP, tm // ATTN_TQ, HEAD_DIM, ATTN_TQ),
                                lambda i, j: (jnp.clip(j - 1, 0, N_Q_TILES - 1), i, 0, 0)),
                   pl.BlockSpec((N_KV_HEADS, tm, HEAD_DIM), lambda i, j: (0, i, 0)),
                   pl.BlockSpec((N_KV_HEADS, tm // ATTN_TK, HEAD_DIM, ATTN_TK), lambda i, j: (0, i, 0, 0))],
        out_shape=[jax.ShapeDtypeStruct((N_Q_HEADS, t // ATTN_TQ, HEAD_DIM, ATTN_TQ), BF16),
                   jax.ShapeDtypeStruct((N_KV_HEADS, t, HEAD_DIM), BF16),
                   jax.ShapeDtypeStruct((N_KV_HEADS, t // ATTN_TK, HEAD_DIM, ATTN_TK), BF16)],
        scratch_shapes=[pltpu.VMEM((2, tm, QKV_TN), F32)],
        compiler_params=_cparams(("parallel", "arbitrary")),
        name="inproj_qkv",
    )(a, w_in, q_norm.reshape(1, HEAD_DIM), k_norm.reshape(1, HEAD_DIM), cos_full, sin_signed)


def _inproj_conv_kernel(a_ref, wb_ref, wc_ref, wh_ref, gb_ref, u_ref):
    a = a_ref[...]
    gb_ref[...] = jnp.dot(a, wb_ref[...], preferred_element_type=F32).astype(gb_ref.dtype)
    c = jnp.dot(a, wc_ref[...], preferred_element_type=F32)
    h = jnp.dot(a, wh_ref[...], preferred_element_type=F32)
    u_ref[...] = (c * h).astype(u_ref.dtype)


def inproj_conv(a, w_in, tm=IN_PROJ_TM, tn=QKV_TN):
    t, d = a.shape
    tm = min(tm, t)
    off_b = (ATTN_WIDTH + 2 * KV_WIDTH) // tn
    off_c = off_b + CONV_WIDTH // tn
    off_h = off_c + CONV_WIDTH // tn
    out = jax.ShapeDtypeStruct((t, CONV_WIDTH), BF16)
    return pl.pallas_call(
        _inproj_conv_kernel,
        grid=(t // tm, CONV_WIDTH // tn),
        in_specs=[pl.BlockSpec((tm, d), lambda i, j: (i, 0)),
                  pl.BlockSpec((d, tn), lambda i, j: (0, off_b + j)),
                  pl.BlockSpec((d, tn), lambda i, j: (0, off_c + j)),
                  pl.BlockSpec((d, tn), lambda i, j: (0, off_h + j))],
        out_specs=[pl.BlockSpec((tm, tn), lambda i, j: (i, j)),
                   pl.BlockSpec((tm, tn), lambda i, j: (i, j))],
        out_shape=[out, out],
        compiler_params=_cparams(("parallel", "arbitrary")),
        name="inproj_conv",
    )(a, w_in, w_in, w_in)


def _attn_kernel(qt_ref, k_ref, vt_ref, o_ref, ksq_ref, m_ref, l_ref, acc_ref, s_ref, p_ref, alpha_ref):
    group, n_q_tiles, _, tq = qt_ref.shape
    n_heads = group * n_q_tiles
    n_chunks = vt_ref.shape[1]
    tk = vt_ref.shape[3]
    slabs = [pl.ds(r, ATTN_SLAB) for r in range(0, tk, ATTN_SLAB)]

    def q_tile(h):
        return qt_ref[h % group, h // group]

    def key_chunk(c):
        return k_ref[0, pl.ds(pl.multiple_of(c * tk, tk), tk), :]

    @pl.when(pl.program_id(2) == 0)
    def _():
        def chunk_max(c, best):
            kc = key_chunk(c).astype(F32)
            return jnp.maximum(best, jnp.max(jnp.sum(kc * kc, axis=1, keepdims=True)))
        ksq_ref[0] = lax.fori_loop(0, n_chunks, chunk_max, jnp.float32(0.0))

    qsq = jnp.float32(0.0)
    for h in range(n_heads):
        q = q_tile(h).astype(F32)
        qsq = jnp.maximum(qsq, jnp.max(jnp.sum(q * q, axis=0, keepdims=True)))
    scores_bounded = qsq * ksq_ref[0] <= SCORE_BOUND_LOG2 * SCORE_BOUND_LOG2

    l_ref[...] = jnp.zeros(l_ref.shape, F32)
    acc_ref[...] = jnp.zeros(acc_ref.shape, F32)
    p_ref[...] = jnp.zeros(p_ref.shape, p_ref.dtype)

    def pipeline(stage_scores, stage_values, unroll):
        unroll = unroll if n_chunks % unroll == 0 else 1

        @pl.loop(0, n_chunks // unroll)
        def _(step):
            for sub in range(unroll):
                c = step * unroll + sub
                for h in range(n_heads):
                    slot, other = h % 2, (h + 1) % 2
                    stage_scores(c, h, slot)
                    if h > 0:
                        stage_values(c, h - 1, other)
                    else:
                        stage_values(jnp.maximum(c - 1, 0), n_heads - 1, other)
        stage_values(n_chunks - 1, n_heads - 1, (n_heads - 1) % 2)

    def unshifted():
        def probs(c, h, slot):
            s = jnp.dot(key_chunk(c), q_tile(h), preferred_element_type=F32)
            slab_sum = jnp.zeros((ATTN_SLAB, s.shape[1]), F32)
            for r in range(0, tk, ATTN_SLAB):
                p = jnp.exp2(s[r:r + ATTN_SLAB])
                slab_sum = slab_sum + p
                p_ref[slot, pl.ds(r, ATTN_SLAB), :] = p.astype(p_ref.dtype)
            l_ref[h] = l_ref[h] + jnp.sum(slab_sum, axis=0, keepdims=True)

        def values(c, h, slot):
            acc_ref[h] = acc_ref[h] + jnp.dot(vt_ref[0, c], p_ref[slot], preferred_element_type=F32)

        pipeline(probs, values, ATTN_CHUNK_UNROLL)

    def running_max():
        def scores(c, h, slot):
            s_ref[slot] = jnp.dot(key_chunk(c), q_tile(h), preferred_element_type=F32)

        def softmax(h, slot):
            m_old = m_ref[h]
            slab_max = s_ref[slot, slabs[0], :]
            for rows in slabs[1:]:
                slab_max = jnp.maximum(slab_max, s_ref[slot, rows, :])
            m_new = jnp.maximum(m_old, jnp.max(slab_max, axis=0, keepdims=True))
            alpha = jnp.exp2(m_old - m_new)
            slab_sum = jnp.zeros(slab_max.shape, F32)
            for rows in slabs:
                p = jnp.exp2(s_ref[slot, rows, :] - m_new)
                slab_sum = slab_sum + p
                p_ref[slot, rows, :] = p.astype(p_ref.dtype)
            l_ref[h] = alpha * l_ref[h] + jnp.sum(slab_sum, axis=0, keepdims=True)
            m_ref[h] = m_new
            alpha_ref[slot] = alpha

        def scores_and_softmax(c, h, slot):
            if h + 1 < n_heads:
                scores(c, h + 1, (h + 1) % 2)
            else:
                scores(jnp.minimum(c + 1, n_chunks - 1), 0, (h + 1) % 2)
            softmax(h, slot)

        def values(c, h, slot):
            pv = jnp.dot(vt_ref[0, c], p_ref[slot], preferred_element_type=F32)
            acc_ref[h] = alpha_ref[slot] * acc_ref[h] + pv

        m_ref[...] = jnp.full(m_ref.shape, -jnp.inf, F32)
        alpha_ref[...] = jnp.ones(alpha_ref.shape, F32)
        scores(0, 0, 0)
        pipeline(scores_and_softmax, values, 1)

    lax.cond(scores_bounded, unshifted, running_max)

    for h in range(n_heads):
        out_t = acc_ref[h] / l_ref[h]
        head, tile = h % group, h // group
        o_ref[tile * tq:(tile + 1) * tq, head * HEAD_DIM:(head + 1) * HEAD_DIM] = out_t.T.astype(o_ref.dtype)


def attention(qt, k, vt, batch, seq_len):
    t = k.shape[1]
    q_tiles = min(ATTN_Q_TILES, seq_len // ATTN_TQ)
    nq = seq_len // (ATTN_TQ * q_tiles)
    nk = seq_len // ATTN_TK
    n_units = KV_GROUP * q_tiles
    return pl.pallas_call(
        _attn_kernel,
        grid=(batch, N_KV_HEADS, nq),
        in_specs=[pl.BlockSpec((KV_GROUP, q_tiles, HEAD_DIM, ATTN_TQ), lambda b, g, i: (g, b * nq + i, 0, 0)),
                  pl.BlockSpec((1, seq_len, HEAD_DIM), lambda b, g, i: (g, b, 0)),
                  pl.BlockSpec((1, nk, HEAD_DIM, ATTN_TK), lambda b, g, i: (g, b, 0, 0))],
        out_specs=pl.BlockSpec((ATTN_TQ * q_tiles, KV_GROUP * HEAD_DIM), lambda b, g, i: (b * nq + i, g)),
        out_shape=jax.ShapeDtypeStruct((t, ATTN_WIDTH), BF16),
        scratch_shapes=[pltpu.SMEM((1,), F32),
                        pltpu.VMEM((n_units, 1, ATTN_TQ), F32),
                        pltpu.VMEM((n_units, 1, ATTN_TQ), F32),
                        pltpu.VMEM((n_units, HEAD_DIM, ATTN_TQ), F32),
                        pltpu.VMEM((2, ATTN_TK, ATTN_TQ), F32),
                        pltpu.VMEM((2, ATTN_TK, ATTN_TQ), BF16),
                        pltpu.VMEM((2, 1, ATTN_TQ), F32)],
        compiler_params=_cparams(("parallel", "parallel", "arbitrary")),
        name="attention",
    )(qt, k, vt)


def _premix_kernel(attn_ref, gb_ref, u_ref, up_ref, un_ref, cw_ref, ga_ref, gc_ref, o_ref, *, n_seq_tiles):
    i = pl.program_id(0)
    tm = u_ref.shape[0]
    o_ref[:, :ATTN_WIDTH] = (_rms(attn_ref[...].astype(F32)) * ga_ref[...]).astype(o_ref.dtype)

    u = u_ref[...].astype(F32)
    first = (i % n_seq_tiles) == 0
    last = (i % n_seq_tiles) == n_seq_tiles - 1
    prev_row = jnp.where(first, 0.0, up_ref[BF16_SUBLANES - 1:BF16_SUBLANES, :].astype(F32))
    next_row = jnp.where(last, 0.0, un_ref[0:1, :].astype(F32))
    row = lax.broadcasted_iota(jnp.int32, u.shape, 0)
    u_prev = jnp.where(row == 0, prev_row, pltpu.roll(u, 1, axis=0))
    u_next = jnp.where(row == tm - 1, next_row, pltpu.roll(u, tm - 1, axis=0))
    cw = cw_ref[...]
    conv = gb_ref[...].astype(F32) * (cw[0:1] * u_prev + cw[1:2] * u + cw[2:3] * u_next)
    o_ref[:, ATTN_WIDTH:] = (_rms(conv) * gc_ref[...]).astype(o_ref.dtype)


def premix(attn, gate_b, u, conv_w, g_attn, g_conv, seq_len, tm=ROW_TILE):
    t = attn.shape[0]
    tm = min(tm, seq_len)
    n_seq_tiles = seq_len // tm
    halo = tm // BF16_SUBLANES
    n_halo = t // BF16_SUBLANES
    row = lambda w: pl.BlockSpec((tm, w), lambda i: (i, 0))
    vec = lambda w: pl.BlockSpec((1, w), lambda i: (0, 0))
    return pl.pallas_call(
        functools.partial(_premix_kernel, n_seq_tiles=n_seq_tiles),
        grid=(t // tm,),
        in_specs=[row(ATTN_WIDTH), row(CONV_WIDTH), row(CONV_WIDTH),
                  pl.BlockSpec((BF16_SUBLANES, CONV_WIDTH), lambda i: (jnp.maximum(i * halo - 1, 0), 0)),
                  pl.BlockSpec((BF16_SUBLANES, CONV_WIDTH),
                               lambda i: (jnp.minimum((i + 1) * halo, n_halo - 1), 0)),
                  pl.BlockSpec((3, CONV_WIDTH), lambda i: (0, 0)),
                  vec(ATTN_WIDTH), vec(CONV_WIDTH)],
        out_specs=row(ATTN_WIDTH + CONV_WIDTH),
        out_shape=jax.ShapeDtypeStruct((t, ATTN_WIDTH + CONV_WIDTH), BF16),
        compiler_params=_cparams(("parallel",)),
        name="premix",
    )(attn, gate_b, u, u, u, conv_w, g_attn.reshape(1, -1), g_conv.reshape(1, -1))


def _rope_tables(seq_len):
    rows = seq_len // GRID_W
    inv = 1.0 / (ROPE_THETA ** (jnp.arange(0, ROPE_AXIS_DIM, 2, dtype=F32) / ROPE_AXIS_DIM))
    row_ang = jnp.arange(rows, dtype=F32)[:, None] * inv
    col_ang = jnp.arange(GRID_W, dtype=F32)[:, None] * inv
    ang = jnp.concatenate([
        jnp.broadcast_to(row_ang[:, None, :], (rows, GRID_W, inv.shape[0])),
        jnp.broadcast_to(col_ang[None, :, :], (rows, GRID_W, inv.shape[0])),
    ], axis=-1).reshape(seq_len, ROPE_AXIS_DIM)
    cos, sin = jnp.cos(ang), jnp.sin(ang)
    return jnp.concatenate([cos, cos], axis=-1), jnp.concatenate([-sin, sin], axis=-1)


def _prep_ffn(w_gate, w_up, w_down):
    return w_gate, w_up, w_down.astype(BF16)


def _ffn(hn, x, w, g_post, g_next):
    wg, wu, wd = w
    h = gate_up(hn, wg, wu)
    y = matmul(h, wd, *DOWN_TILE)
    return resid_norm_rows(y, x, g_post, FFN_RES_SCALE, g_next)


def _trunk(x3d, p):
    b, s, d = x3d.shape
    x = x3d.reshape(b * s, d)
    cos_full, sin_signed = _rope_tables(s)
    hn = rms_norm_rows(x, p["ffn1_pre"])
    x, hn = _ffn(hn, x, p["ffn1"], p["ffn1_post"], p["mix_pre"])
    qt, k, vt = inproj_qkv(hn, p["w_in"], p["q_norm"], p["k_norm"], cos_full, sin_signed, s)
    gate_b, u = inproj_conv(hn, p["w_in"])
    attn = attention(qt, k, vt, b, s)
    mix_in = premix(attn, gate_b, u, p["conv_w"], p["attn_out_norm"], p["conv_out_norm"], s)
    x, hn = matmul_resid_norm(mix_in, p["w_out"], x, p["mix_post"], 1.0, p["ffn2_pre"], *OUT_PROJ_TILE)
    x, _ = _ffn(hn, x, p["ffn2"], p["ffn2_post"], None)
    return x.reshape(b, s, d)


def kernel(x_prompt, x_sample, ffn1_pre, ffn1_post, w1_gate, w1_up, w1_down, mix_pre, mix_post, w_in, q_norm, k_norm, conv_w, attn_out_norm, conv_out_norm, w_out, ffn2_pre, ffn2_post, w2_gate, w2_up, w2_down):
    p = {
        "ffn1_pre": ffn1_pre[0], "ffn1_post": ffn1_post[0],
        "ffn1": _prep_ffn(w1_gate[0], w1_up[0], w1_down[0]),
        "mix_pre": mix_pre[0], "mix_post": mix_post[0],
        "w_in": w_in[0].astype(BF16), "q_norm": q_norm[0], "k_norm": k_norm[0],
        "conv_w": conv_w[0], "attn_out_norm": attn_out_norm[0], "conv_out_norm": conv_out_norm[0],
        "w_out": w_out[0].astype(BF16),
        "ffn2_pre": ffn2_pre[0], "ffn2_post": ffn2_post[0],
        "ffn2": _prep_ffn(w2_gate[0], w2_up[0], w2_down[0]),
    }
    return _trunk(x_prompt, p), _trunk(x_sample, p)
```

```python
import functools

import jax
import jax.numpy as jnp
from jax import lax
from jax.experimental import pallas as pl
from jax.experimental.pallas import tpu as pltpu

F32 = jnp.float32
BF16 = jnp.bfloat16

NORM_EPS = 1e-6
FFN_RES_SCALE = 0.5
HEAD_DIM = 128
N_Q_HEADS = 16
N_KV_HEADS = 4
KV_GROUP = N_Q_HEADS // N_KV_HEADS
ATTN_WIDTH = N_Q_HEADS * HEAD_DIM
KV_WIDTH = N_KV_HEADS * HEAD_DIM
CONV_WIDTH = 2048
GRID_W = 64
ROPE_THETA = 10000.0
ROPE_AXIS_DIM = HEAD_DIM // 2

BF16_SUBLANES = 16
VMEM_LIMIT_BYTES = 56 * 1024 * 1024
GATE_UP_TILE = (4096, 256)
GATE_UP_SUB_ROWS = 1024
DOWN_TILE = (512, 512)
OUT_PROJ_TILE = (1024, 1024)
IN_PROJ_TM = 1024
ROW_TILE = 256


def _cparams(sem):
    return pltpu.CompilerParams(dimension_semantics=sem, vmem_limit_bytes=VMEM_LIMIT_BYTES)


def _rms(x):
    return x * lax.rsqrt(jnp.mean(x * x, axis=-1, keepdims=True) + NORM_EPS)


def _norm_kernel(x_ref, g_ref, o_ref):
    o_ref[...] = (_rms(x_ref[...]) * g_ref[...]).astype(o_ref.dtype)


def rms_norm_rows(x, g, tm=ROW_TILE):
    t, d = x.shape
    tm = min(tm, t)
    assert t % tm == 0, (t, tm)
    return pl.pallas_call(
        _norm_kernel,
        grid=(t // tm,),
        in_specs=[pl.BlockSpec((tm, d), lambda i: (i, 0)),
                  pl.BlockSpec((1, d), lambda i: (0, 0))],
        out_specs=pl.BlockSpec((tm, d), lambda i: (i, 0)),
        out_shape=jax.ShapeDtypeStruct((t, d), BF16),
        compiler_params=_cparams(("parallel",)),
        name="rms_norm_rows",
    )(x, g.reshape(1, d))


def _resid_norm_kernel(y_ref, x_ref, gp_ref, gn_ref, xo_ref, ho_ref, *, scale):
    x_new = x_ref[...] + scale * (_rms(y_ref[...]) * gp_ref[...])
    xo_ref[...] = x_new
    ho_ref[...] = (_rms(x_new) * gn_ref[...]).astype(ho_ref.dtype)


def _resid_kernel(y_ref, x_ref, gp_ref, xo_ref, *, scale):
    xo_ref[...] = x_ref[...] + scale * (_rms(y_ref[...]) * gp_ref[...])


def resid_norm_rows(y, x, g_post, scale, g_next=None, tm=ROW_TILE):
    t, d = x.shape
    tm = min(tm, t)
    assert t % tm == 0 and y.shape == x.shape, (y.shape, x.shape, tm)
    row = pl.BlockSpec((tm, d), lambda i: (i, 0))
    vec = pl.BlockSpec((1, d), lambda i: (0, 0))
    if g_next is None:
        return pl.pallas_call(
            functools.partial(_resid_kernel, scale=scale),
            grid=(t // tm,),
            in_specs=[row, row, vec],
            out_specs=row,
            out_shape=jax.ShapeDtypeStruct((t, d), F32),
            compiler_params=_cparams(("parallel",)),
            name="resid_rows",
        )(y, x, g_post.reshape(1, d)), None
    return pl.pallas_call(
        functools.partial(_resid_norm_kernel, scale=scale),
        grid=(t // tm,),
        in_specs=[row, row, vec, vec],
        out_specs=[row, row],
        out_shape=[jax.ShapeDtypeStruct((t, d), F32), jax.ShapeDtypeStruct((t, d), BF16)],
        compiler_params=_cparams(("parallel",)),
        name="resid_norm_rows",
    )(y, x, g_post.reshape(1, d), g_next.reshape(1, d))


def _gateup_kernel(a_ref, wg_ref, wu_ref, o_ref):
    sub = min(GATE_UP_SUB_ROWS, a_ref.shape[0])
    wg = wg_ref[...].astype(a_ref.dtype)
    wu = wu_ref[...].astype(a_ref.dtype)
    for r in range(0, a_ref.shape[0], sub):
        a = a_ref[r:r + sub, :]
        g = jnp.dot(a, wg, preferred_element_type=F32)
        u = jnp.dot(a, wu, preferred_element_type=F32)
        o_ref[r:r + sub, :] = (g * jax.nn.sigmoid(g) * u).astype(o_ref.dtype)


def gate_up(a, wg, wu):
    t, d = a.shape
    f = wg.shape[1]
    tm, tn = min(GATE_UP_TILE[0], t), GATE_UP_TILE[1]
    assert t % tm == 0 and f % tn == 0 and tm % min(GATE_UP_SUB_ROWS, tm) == 0, (t, f, tm, tn)
    return pl.pallas_call(
        _gateup_kernel,
        grid=(t // tm, f // tn),
        in_specs=[pl.BlockSpec((tm, d), lambda i, j: (i, 0), pipeline_mode=pl.Buffered(1)),
                  pl.BlockSpec((d, tn), lambda i, j: (0, j)),
                  pl.BlockSpec((d, tn), lambda i, j: (0, j))],
        out_specs=pl.BlockSpec((tm, tn), lambda i, j: (i, j)),
        out_shape=jax.ShapeDtypeStruct((t, f), BF16),
        compiler_params=_cparams(("parallel", "arbitrary")),
        name="gate_up",
    )(a, wg, wu)


def _mm_kernel(a_ref, w_ref, o_ref):
    o_ref[...] = jnp.dot(a_ref[...], w_ref[...], preferred_element_type=F32)


def matmul(a, w, tm, tn):
    t, kd = a.shape
    n = w.shape[1]
    tm, tn = min(tm, t), min(tn, n)
    assert t % tm == 0 and n % tn == 0 and w.shape[0] == kd, (a.shape, w.shape, tm, tn)
    return pl.pallas_call(
        _mm_kernel,
        grid=(t // tm, n // tn),
        in_specs=[pl.BlockSpec((tm, kd), lambda i, j: (i, 0)),
                  pl.BlockSpec((kd, tn), lambda i, j: (0, j))],
        out_specs=pl.BlockSpec((tm, tn), lambda i, j: (i, j)),
        out_shape=jax.ShapeDtypeStruct((t, n), F32),
        compiler_params=_cparams(("parallel", "arbitrary")),
        name="matmul_fullk",
    )(a, w)


QKV_TN = KV_GROUP * HEAD_DIM
N_Q_TILES = ATTN_WIDTH // QKV_TN
ATTN_TQ = 256
ATTN_TK = 1024
ATTN_SLAB = 64
ATTN_Q_TILES = 4
ATTN_CHUNK_UNROLL = 4
SCORE_BOUND_LOG2 = 40.0
Q_SCALE = (HEAD_DIM ** -0.5) * 1.4426950408889634


def _inproj_qkv_kernel(a_ref, w_ref, qg_ref, kg_ref, cos_ref, sin_ref, qt_ref, k_ref, vt_ref, raw_ref):
    j = pl.program_id(1)
    tm = a_ref.shape[0]
    n_steps = N_Q_TILES + 3

    def matmul(slot):
        raw_ref[slot] = jnp.dot(a_ref[...], w_ref[...], preferred_element_type=F32)

    def norm_rope(y, gain):
        y = _rms(y) * gain
        return y * cos_ref[...] + pltpu.roll(y, ROPE_AXIS_DIM, axis=1) * sin_ref[...]

    def store_transposed(dst_ref, h, y, width):
        yt = y.T.astype(dst_ref.dtype)
        for c in range(tm // width):
            dst_ref[h, c] = yt[:, c * width:(c + 1) * width]

    def epilogue(tile, slot):
        for h in range(KV_GROUP):
            y = raw_ref[slot, :, h * HEAD_DIM:(h + 1) * HEAD_DIM]
            if tile < N_Q_TILES:
                store_transposed(qt_ref, h, norm_rope(y, qg_ref[...]) * Q_SCALE, ATTN_TQ)
            elif tile == N_Q_TILES:
                k_ref[h] = norm_rope(y, kg_ref[...]).astype(k_ref.dtype)
            else:
                store_transposed(vt_ref, h, y, ATTN_TK)

    for step in range(n_steps):
        @pl.when(j == step)
        def _(step=step):
            if step < n_steps - 1:
                matmul(step % 2)
            if step > 0:
                epilogue(step - 1, (step - 1) % 2)


def inproj_qkv(a, w_in, q_norm, k_norm, cos_full, sin_signed, seq_len, tm=IN_PROJ_TM):
    t, d = a.shape
    tm = min(tm, seq_len)
    n_seq_tiles = seq_len // tm
    assert seq_len % tm == 0 and t % seq_len == 0 and tm % ATTN_TQ == 0 and tm % ATTN_TK == 0, (t, seq_len, tm)
    return pl.pallas_call(
        _inproj_qkv_kernel,
        grid=(t // tm, N_Q_TILES + 3),
        in_specs=[pl.BlockSpec((tm, d), lambda i, j: (i, 0)),
                  pl.BlockSpec((d, QKV_TN), lambda i, j: (0, jnp.minimum(j, N_Q_TILES + 1))),
                  pl.BlockSpec((1, HEAD_DIM), lambda i, j: (0, 0)),
                  pl.BlockSpec((1, HEAD_DIM), lambda i, j: (0, 0)),
                  pl.BlockSpec((tm, HEAD_DIM), lambda i, j: (i % n_seq_tiles, 0)),
                  pl.BlockSpec((tm, HEAD_DIM), lambda i, j: (i % n_seq_tiles, 0))],
        out_specs=[pl.BlockSpec((KV_GROUP, tm // ATTN_TQ, HEAD_DIM, ATTN_TQ),
                                lambda i, j: (jnp.clip(j - 1, 0, N_Q_TILES - 1), i, 0, 0)),
                   pl.BlockSpec((N_KV_HEADS, tm, HEAD_DIM), lambda i, j: (0, i, 0)),
                   pl.BlockSpec((N_KV_HEADS, tm // ATTN_TK, HEAD_DIM, ATTN_TK), lambda i, j: (0, i, 0, 0))],
        out_shape=[jax.ShapeDtypeStruct((N_Q_HEADS, t // ATTN_TQ, HEAD_DIM, ATTN_TQ), BF16),
                   jax.ShapeDtypeStruct((N_KV_HEADS, t, HEAD_DIM), BF16),
                   jax.ShapeDtypeStruct((N_KV_HEADS, t // ATTN_TK, HEAD_DIM, ATTN_TK), BF16)],
        scratch_shapes=[pltpu.VMEM((2, tm, QKV_TN), F32)],
        compiler_params=_cparams(("parallel", "arbitrary")),
        name="inproj_qkv",
    )(a, w_in, q_norm.reshape(1, HEAD_DIM), k_norm.reshape(1, HEAD_DIM), cos_full, sin_signed)


def _inproj_conv_kernel(a_ref, wb_ref, wc_ref, wh_ref, gb_ref, u_ref):
    a = a_ref[...]
    gb_ref[...] = jnp.dot(a, wb_ref[...], preferred_element_type=F32).astype(gb_ref.dtype)
    c = jnp.dot(a, wc_ref[...], preferred_element_type=F32)
    h = jnp.dot(a, wh_ref[...], preferred_element_type=F32)
    u_ref[...] = (c * h).astype(u_ref.dtype)


def inproj_conv(a, w_in, tm=IN_PROJ_TM, tn=QKV_TN):
    t, d = a.shape
    tm = min(tm, t)
    assert t % tm == 0 and CONV_WIDTH % tn == 0 and (ATTN_WIDTH + 2 * KV_WIDTH) % tn == 0, (t, tm, tn)
    off_b = (ATTN_WIDTH + 2 * KV_WIDTH) // tn
    off_c = off_b + CONV_WIDTH // tn
    off_h = off_c + CONV_WIDTH // tn
    out = jax.ShapeDtypeStruct((t, CONV_WIDTH), BF16)
    return pl.pallas_call(
        _inproj_conv_kernel,
        grid=(t // tm, CONV_WIDTH // tn),
        in_specs=[pl.BlockSpec((tm, d), lambda i, j: (i, 0)),
                  pl.BlockSpec((d, tn), lambda i, j: (0, off_b + j)),
                  pl.BlockSpec((d, tn), lambda i, j: (0, off_c + j)),
                  pl.BlockSpec((d, tn), lambda i, j: (0, off_h + j))],
        out_specs=[pl.BlockSpec((tm, tn), lambda i, j: (i, j)),
                   pl.BlockSpec((tm, tn), lambda i, j: (i, j))],
        out_shape=[out, out],
        compiler_params=_cparams(("parallel", "arbitrary")),
        name="inproj_conv",
    )(a, w_in, w_in, w_in)


def _attn_kernel(qt_ref, k_ref, vt_ref, o_ref, ksq_ref, m_ref, l_ref, acc_ref, s_ref, p_ref, alpha_ref):
    group, n_q_tiles, _, tq = qt_ref.shape
    n_heads = group * n_q_tiles
    n_chunks = vt_ref.shape[1]
    tk = vt_ref.shape[3]
    slabs = [pl.ds(r, ATTN_SLAB) for r in range(0, tk, ATTN_SLAB)]

    def q_tile(h):
        return qt_ref[h % group, h // group]

    def key_chunk(c):
        return k_ref[0, pl.ds(pl.multiple_of(c * tk, tk), tk), :]

    @pl.when(pl.program_id(2) == 0)
    def _():
        def chunk_max(c, best):
            kc = key_chunk(c).astype(F32)
            return jnp.maximum(best, jnp.max(jnp.sum(kc * kc, axis=1, keepdims=True)))
        ksq_ref[0] = lax.fori_loop(0, n_chunks, chunk_max, jnp.float32(0.0))

    qsq = jnp.float32(0.0)
    for h in range(n_heads):
        q = q_tile(h).astype(F32)
        qsq = jnp.maximum(qsq, jnp.max(jnp.sum(q * q, axis=0, keepdims=True)))
    scores_bounded = qsq * ksq_ref[0] <= SCORE_BOUND_LOG2 * SCORE_BOUND_LOG2

    l_ref[...] = jnp.zeros(l_ref.shape, F32)
    acc_ref[...] = jnp.zeros(acc_ref.shape, F32)
    p_ref[...] = jnp.zeros(p_ref.shape, p_ref.dtype)

    def pipeline(stage_scores, stage_values, unroll):
        unroll = unroll if n_chunks % unroll == 0 else 1

        @pl.loop(0, n_chunks // unroll)
        def _(step):
            for sub in range(unroll):
                c = step * unroll + sub
                for h in range(n_heads):
                    slot, other = h % 2, (h + 1) % 2
                    stage_scores(c, h, slot)
                    if h > 0:
                        stage_values(c, h - 1, other)
                    else:
                        stage_values(jnp.maximum(c - 1, 0), n_heads - 1, other)
        stage_values(n_chunks - 1, n_heads - 1, (n_heads - 1) % 2)

    def unshifted():
        def probs(c, h, slot):
            s = jnp.dot(key_chunk(c), q_tile(h), preferred_element_type=F32)
            slab_sum = jnp.zeros((ATTN_SLAB, s.shape[1]), F32)
            for r in range(0, tk, ATTN_SLAB):
                p = jnp.exp2(s[r:r + ATTN_SLAB])
                slab_sum = slab_sum + p
                p_ref[slot, pl.ds(r, ATTN_SLAB), :] = p.astype(p_ref.dtype)
            l_ref[h] = l_ref[h] + jnp.sum(slab_sum, axis=0, keepdims=True)

        def values(c, h, slot):
            acc_ref[h] = acc_ref[h] + jnp.dot(vt_ref[0, c], p_ref[slot], preferred_element_type=F32)

        pipeline(probs, values, ATTN_CHUNK_UNROLL)

    def running_max():
        def scores(c, h, slot):
            s_ref[slot] = jnp.dot(key_chunk(c), q_tile(h), preferred_element_type=F32)

        def softmax(h, slot):
            m_old = m_ref[h]
            slab_max = s_ref[slot, slabs[0], :]
            for rows in slabs[1:]:
                slab_max = jnp.maximum(slab_max, s_ref[slot, rows, :])
            m_new = jnp.maximum(m_old, jnp.max(slab_max, axis=0, keepdims=True))
            alpha = jnp.exp2(m_old - m_new)
            slab_sum = jnp.zeros(slab_max.shape, F32)
            for rows in slabs:
                p = jnp.exp2(s_ref[slot, rows, :] - m_new)
                slab_sum = slab_sum + p
                p_ref[slot, rows, :] = p.astype(p_ref.dtype)
            l_ref[h] = alpha * l_ref[h] + jnp.sum(slab_sum, axis=0, keepdims=True)
            m_ref[h] = m_new
            alpha_ref[slot] = alpha

        def scores_and_softmax(c, h, slot):
            if h + 1 < n_heads:
                scores(c, h + 1, (h + 1) % 2)
            else:
                scores(jnp.minimum(c + 1, n_chunks - 1), 0, (h + 1) % 2)
            softmax(h, slot)

        def values(c, h, slot):
            pv = jnp.dot(vt_ref[0, c], p_ref[slot], preferred_element_type=F32)
            acc_ref[h] = alpha_ref[slot] * acc_ref[h] + pv

        m_ref[...] = jnp.full(m_ref.shape, -jnp.inf, F32)
        alpha_ref[...] = jnp.ones(alpha_ref.shape, F32)
        scores(0, 0, 0)
        pipeline(scores_and_softmax, values, 1)

    lax.cond(scores_bounded, unshifted, running_max)

    for h in range(n_heads):
        out_t = acc_ref[h] / l_ref[h]
        head, tile = h % group, h // group
        o_ref[tile * tq:(tile + 1) * tq, head * HEAD_DIM:(head + 1) * HEAD_DIM] = out_t.T.astype(o_ref.dtype)


def attention(qt, k, vt, batch, seq_len):
    t = k.shape[1]
    q_tiles = min(ATTN_Q_TILES, seq_len // ATTN_TQ)
    nq = seq_len // (ATTN_TQ * q_tiles)
    nk = seq_len // ATTN_TK
    n_units = KV_GROUP * q_tiles
    assert seq_len % (ATTN_TQ * q_tiles) == 0 and seq_len % ATTN_TK == 0 and n_units % 2 == 0, (seq_len, q_tiles)
    assert t == batch * seq_len and ATTN_TK % ATTN_SLAB == 0, (t, batch, seq_len)
    return pl.pallas_call(
        _attn_kernel,
        grid=(batch, N_KV_HEADS, nq),
        in_specs=[pl.BlockSpec((KV_GROUP, q_tiles, HEAD_DIM, ATTN_TQ), lambda b, g, i: (g, b * nq + i, 0, 0)),
                  pl.BlockSpec((1, seq_len, HEAD_DIM), lambda b, g, i: (g, b, 0)),
                  pl.BlockSpec((1, nk, HEAD_DIM, ATTN_TK), lambda b, g, i: (g, b, 0, 0))],
        out_specs=pl.BlockSpec((ATTN_TQ * q_tiles, KV_GROUP * HEAD_DIM), lambda b, g, i: (b * nq + i, g)),
        out_shape=jax.ShapeDtypeStruct((t, ATTN_WIDTH), BF16),
        scratch_shapes=[pltpu.SMEM((1,), F32),
                        pltpu.VMEM((n_units, 1, ATTN_TQ), F32),
                        pltpu.VMEM((n_units, 1, ATTN_TQ), F32),
                        pltpu.VMEM((n_units, HEAD_DIM, ATTN_TQ), F32),
                        pltpu.VMEM((2, ATTN_TK, ATTN_TQ), F32),
                        pltpu.VMEM((2, ATTN_TK, ATTN_TQ), BF16),
                        pltpu.VMEM((2, 1, ATTN_TQ), F32)],
        compiler_params=_cparams(("parallel", "parallel", "arbitrary")),
        name="attention",
    )(qt, k, vt)


def _premix_kernel(attn_ref, gb_ref, u_ref, up_ref, un_ref, cw_ref, ga_ref, gc_ref, o_ref, *, n_seq_tiles):
    i = pl.program_id(0)
    tm = u_ref.shape[0]
    o_ref[:, :ATTN_WIDTH] = (_rms(attn_ref[...].astype(F32)) * ga_ref[...]).astype(o_ref.dtype)

    u = u_ref[...].astype(F32)
    first = (i % n_seq_tiles) == 0
    last = (i % n_seq_tiles) == n_seq_tiles - 1
    prev_row = jnp.where(first, 0.0, up_ref[BF16_SUBLANES - 1:BF16_SUBLANES, :].astype(F32))
    next_row = jnp.where(last, 0.0, un_ref[0:1, :].astype(F32))
    row = lax.broadcasted_iota(jnp.int32, u.shape, 0)
    u_prev = jnp.where(row == 0, prev_row, pltpu.roll(u, 1, axis=0))
    u_next = jnp.where(row == tm - 1, next_row, pltpu.roll(u, tm - 1, axis=0))
    cw = cw_ref[...]
    conv = gb_ref[...].astype(F32) * (cw[0:1] * u_prev + cw[1:2] * u + cw[2:3] * u_next)
    o_ref[:, ATTN_WIDTH:] = (_rms(conv) * gc_ref[...]).astype(o_ref.dtype)


def premix(attn, gate_b, u, conv_w, g_attn, g_conv, seq_len, tm=ROW_TILE):
    t = attn.shape[0]
    tm = min(tm, seq_len)
    n_seq_tiles = seq_len // tm
    assert seq_len % tm == 0 and t % seq_len == 0 and tm % BF16_SUBLANES == 0, (t, seq_len, tm)
    halo = tm // BF16_SUBLANES
    n_halo = t // BF16_SUBLANES
    row = lambda w: pl.BlockSpec((tm, w), lambda i: (i, 0))
    vec = lambda w: pl.BlockSpec((1, w), lambda i: (0, 0))
    return pl.pallas_call(
        functools.partial(_premix_kernel, n_seq_tiles=n_seq_tiles),
        grid=(t // tm,),
        in_specs=[row(ATTN_WIDTH), row(CONV_WIDTH), row(CONV_WIDTH),
                  pl.BlockSpec((BF16_SUBLANES, CONV_WIDTH), lambda i: (jnp.maximum(i * halo - 1, 0), 0)),
                  pl.BlockSpec((BF16_SUBLANES, CONV_WIDTH),
                               lambda i: (jnp.minimum((i + 1) * halo, n_halo - 1), 0)),
                  pl.BlockSpec((3, CONV_WIDTH), lambda i: (0, 0)),
                  vec(ATTN_WIDTH), vec(CONV_WIDTH)],
        out_specs=row(ATTN_WIDTH + CONV_WIDTH),
        out_shape=jax.ShapeDtypeStruct((t, ATTN_WIDTH + CONV_WIDTH), BF16),
        compiler_params=_cparams(("parallel",)),
        name="premix",
    )(attn, gate_b, u, u, u, conv_w, g_attn.reshape(1, -1), g_conv.reshape(1, -1))


def _rope_tables(seq_len):
    rows = seq_len // GRID_W
    inv = 1.0 / (ROPE_THETA ** (jnp.arange(0, ROPE_AXIS_DIM, 2, dtype=F32) / ROPE_AXIS_DIM))
    row_ang = jnp.arange(rows, dtype=F32)[:, None] * inv
    col_ang = jnp.arange(GRID_W, dtype=F32)[:, None] * inv
    ang = jnp.concatenate([
        jnp.broadcast_to(row_ang[:, None, :], (rows, GRID_W, inv.shape[0])),
        jnp.broadcast_to(col_ang[None, :, :], (rows, GRID_W, inv.shape[0])),
    ], axis=-1).reshape(seq_len, ROPE_AXIS_DIM)
    cos, sin = jnp.cos(ang), jnp.sin(ang)
    return jnp.concatenate([cos, cos], axis=-1), jnp.concatenate([-sin, sin], axis=-1)


def _prep_ffn(w_gate, w_up, w_down):
    return w_gate, w_up, w_down.astype(BF16)


def _ffn(hn, x, w, g_post, g_next):
    wg, wu, wd = w
    h = gate_up(hn, wg, wu)
    y = matmul(h, wd, *DOWN_TILE)
    return resid_norm_rows(y, x, g_post, FFN_RES_SCALE, g_next)


def _trunk(x3d, p):
    b, s, d = x3d.shape
    x = x3d.reshape(b * s, d)
    cos_full, sin_signed = _rope_tables(s)
    hn = rms_norm_rows(x, p["ffn1_pre"])
    x, hn = _ffn(hn, x, p["ffn1"], p["ffn1_post"], p["mix_pre"])
    qt, k, vt = inproj_qkv(hn, p["w_in"], p["q_norm"], p["k_norm"], cos_full, sin_signed, s)
    gate_b, u = inproj_conv(hn, p["w_in"])
    attn = attention(qt, k, vt, b, s)
    mix_in = premix(attn, gate_b, u, p["conv_w"], p["attn_out_norm"], p["conv_out_norm"], s)
    y = matmul(mix_in, p["w_out"], *OUT_PROJ_TILE)
    x, hn = resid_norm_rows(y, x, p["mix_post"], 1.0, p["ffn2_pre"])
    x, _ = _ffn(hn, x, p["ffn2"], p["ffn2_post"], None)
    return x.reshape(b, s, d)


def kernel(x_prompt, x_sample, ffn1_pre, ffn1_post, w1_gate, w1_up, w1_down, mix_pre, mix_post, w_in, q_norm, k_norm, conv_w, attn_out_norm, conv_out_norm, w_out, ffn2_pre, ffn2_post, w2_gate, w2_up, w2_down):
    p = {
        "ffn1_pre": ffn1_pre[0], "ffn1_post": ffn1_post[0],
        "ffn1": _prep_ffn(w1_gate[0], w1_up[0], w1_down[0]),
        "mix_pre": mix_pre[0], "mix_post": mix_post[0],
        "w_in": w_in[0].astype(BF16), "q_norm": q_norm[0], "k_norm": k_norm[0],
        "conv_w": conv_w[0], "attn_out_norm": attn_out_norm[0], "conv_out_norm": conv_out_norm[0],
        "w_out": w_out[0].astype(BF16),
        "ffn2_pre": ffn2_pre[0], "ffn2_post": ffn2_post[0],
        "ffn2": _prep_ffn(w2_gate[0], w2_up[0], w2_down[0]),
    }
    return _trunk(x_prompt, p), _trunk(x_sample, p)
```

```python
import functools

import jax
import jax.numpy as jnp
from jax import lax
from jax.experimental import pallas as pl
from jax.experimental.pallas import tpu as pltpu

F32 = jnp.float32
BF16 = jnp.bfloat16

NORM_EPS = 1e-6
FFN_RES_SCALE = 0.5
HEAD_DIM = 128
N_Q_HEADS = 16
N_KV_HEADS = 4
KV_GROUP = N_Q_HEADS // N_KV_HEADS
ATTN_WIDTH = N_Q_HEADS * HEAD_DIM
KV_WIDTH = N_KV_HEADS * HEAD_DIM
CONV_WIDTH = 2048
GRID_W = 64
ROPE_THETA = 10000.0
ROPE_AXIS_DIM = HEAD_DIM // 2

BF16_SUBLANES = 16
VMEM_LIMIT_BYTES = 56 * 1024 * 1024
GATE_UP_TILE = (4096, 256)
GATE_UP_SUB_ROWS = 1024
DOWN_TILE = (512, 512)
OUT_PROJ_TILE = (1024, 1024)
IN_PROJ_TM = 1024
ROW_TILE = 256


def _cparams(sem):
    return pltpu.CompilerParams(dimension_semantics=sem, vmem_limit_bytes=VMEM_LIMIT_BYTES)


def _rms(x):
    return x * lax.rsqrt(jnp.mean(x * x, axis=-1, keepdims=True) + NORM_EPS)


def _norm_kernel(x_ref, g_ref, o_ref):
    o_ref[...] = (_rms(x_ref[...]) * g_ref[...]).astype(o_ref.dtype)


def rms_norm_rows(x, g, tm=ROW_TILE):
    t, d = x.shape
    tm = min(tm, t)
    assert t % tm == 0, (t, tm)
    return pl.pallas_call(
        _norm_kernel,
        grid=(t // tm,),
        in_specs=[pl.BlockSpec((tm, d), lambda i: (i, 0)),
                  pl.BlockSpec((1, d), lambda i: (0, 0))],
        out_specs=pl.BlockSpec((tm, d), lambda i: (i, 0)),
        out_shape=jax.ShapeDtypeStruct((t, d), BF16),
        compiler_params=_cparams(("parallel",)),
        name="rms_norm_rows",
    )(x, g.reshape(1, d))


def _resid_norm_kernel(y_ref, x_ref, gp_ref, gn_ref, xo_ref, ho_ref, *, scale):
    x_new = x_ref[...] + scale * (_rms(y_ref[...]) * gp_ref[...])
    xo_ref[...] = x_new
    ho_ref[...] = (_rms(x_new) * gn_ref[...]).astype(ho_ref.dtype)


def _resid_kernel(y_ref, x_ref, gp_ref, xo_ref, *, scale):
    xo_ref[...] = x_ref[...] + scale * (_rms(y_ref[...]) * gp_ref[...])


def resid_norm_rows(y, x, g_post, scale, g_next=None, tm=ROW_TILE):
    t, d = x.shape
    tm = min(tm, t)
    assert t % tm == 0 and y.shape == x.shape, (y.shape, x.shape, tm)
    row = pl.BlockSpec((tm, d), lambda i: (i, 0))
    vec = pl.BlockSpec((1, d), lambda i: (0, 0))
    if g_next is None:
        return pl.pallas_call(
            functools.partial(_resid_kernel, scale=scale),
            grid=(t // tm,),
            in_specs=[row, row, vec],
            out_specs=row,
            out_shape=jax.ShapeDtypeStruct((t, d), F32),
            compiler_params=_cparams(("parallel",)),
            name="resid_rows",
        )(y, x, g_post.reshape(1, d)), None
    return pl.pallas_call(
        functools.partial(_resid_norm_kernel, scale=scale),
        grid=(t // tm,),
        in_specs=[row, row, vec, vec],
        out_specs=[row, row],
        out_shape=[jax.ShapeDtypeStruct((t, d), F32), jax.ShapeDtypeStruct((t, d), BF16)],
        compiler_params=_cparams(("parallel",)),
        name="resid_norm_rows",
    )(y, x, g_post.reshape(1, d), g_next.reshape(1, d))


def _gateup_kernel(a_ref, wg_ref, wu_ref, o_ref):
    sub = min(GATE_UP_SUB_ROWS, a_ref.shape[0])
    wg = wg_ref[...].astype(a_ref.dtype)
    wu = wu_ref[...].astype(a_ref.dtype)
    for r in range(0, a_ref.shape[0], sub):
        a = a_ref[r:r + sub, :]
        g = jnp.dot(a, wg, preferred_element_type=F32)
        u = jnp.dot(a, wu, preferred_element_type=F32)
        o_ref[r:r + sub, :] = (g * jax.nn.sigmoid(g) * u).astype(o_ref.dtype)


def gate_up(a, wg, wu):
    t, d = a.shape
    f = wg.shape[1]
    tm, tn = min(GATE_UP_TILE[0], t), GATE_UP_TILE[1]
    assert t % tm == 0 and f % tn == 0 and tm % min(GATE_UP_SUB_ROWS, tm) == 0, (t, f, tm, tn)
    return pl.pallas_call(
        _gateup_kernel,
        grid=(t // tm, f // tn),
        in_specs=[pl.BlockSpec((tm, d), lambda i, j: (i, 0), pipeline_mode=pl.Buffered(1)),
                  pl.BlockSpec((d, tn), lambda i, j: (0, j)),
                  pl.BlockSpec((d, tn), lambda i, j: (0, j))],
        out_specs=pl.BlockSpec((tm, tn), lambda i, j: (i, j)),
        out_shape=jax.ShapeDtypeStruct((t, f), BF16),
        compiler_params=_cparams(("parallel", "arbitrary")),
        name="gate_up",
    )(a, wg, wu)


def _mm_kernel(a_ref, w_ref, o_ref):
    o_ref[...] = jnp.dot(a_ref[...], w_ref[...], preferred_element_type=F32)


def matmul(a, w, tm, tn):
    t, kd = a.shape
    n = w.shape[1]
    tm, tn = min(tm, t), min(tn, n)
    assert t % tm == 0 and n % tn == 0 and w.shape[0] == kd, (a.shape, w.shape, tm, tn)
    return pl.pallas_call(
        _mm_kernel,
        grid=(t // tm, n // tn),
        in_specs=[pl.BlockSpec((tm, kd), lambda i, j: (i, 0)),
                  pl.BlockSpec((kd, tn), lambda i, j: (0, j))],
        out_specs=pl.BlockSpec((tm, tn), lambda i, j: (i, j)),
        out_shape=jax.ShapeDtypeStruct((t, n), F32),
        compiler_params=_cparams(("parallel", "arbitrary")),
        name="matmul_fullk",
    )(a, w)


QKV_TN = KV_GROUP * HEAD_DIM
N_Q_TILES = ATTN_WIDTH // QKV_TN
ATTN_TQ = 256
ATTN_TK = 1024
ATTN_SLAB = 64
ATTN_Q_TILES = 4
ATTN_CHUNK_UNROLL = 4
SCORE_BOUND_LOG2 = 40.0
Q_SCALE = (HEAD_DIM ** -0.5) * 1.4426950408889634


def _inproj_qkv_kernel(a_ref, w_ref, qg_ref, kg_ref, cos_ref, sin_ref, qt_ref, k_ref, vt_ref, raw_ref):
    j = pl.program_id(1)
    tm = a_ref.shape[0]
    n_steps = N_Q_TILES + 3

    def matmul(slot):
        raw_ref[slot] = jnp.dot(a_ref[...], w_ref[...], preferred_element_type=F32)

    def norm_rope(y, gain):
        y = _rms(y) * gain
        return y * cos_ref[...] + pltpu.roll(y, ROPE_AXIS_DIM, axis=1) * sin_ref[...]

    def store_transposed(dst_ref, h, y, width):
        yt = y.T.astype(dst_ref.dtype)
        for c in range(tm // width):
            dst_ref[h, c] = yt[:, c * width:(c + 1) * width]

    def epilogue(tile, slot):
        for h in range(KV_GROUP):
            y = raw_ref[slot, :, h * HEAD_DIM:(h + 1) * HEAD_DIM]
            if tile < N_Q_TILES:
                store_transposed(qt_ref, h, norm_rope(y, qg_ref[...]) * Q_SCALE, ATTN_TQ)
            elif tile == N_Q_TILES:
                k_ref[h] = norm_rope(y, kg_ref[...]).astype(k_ref.dtype)
            else:
                store_transposed(vt_ref, h, y, ATTN_TK)

    for step in range(n_steps):
        @pl.when(j == step)
        def _(step=step):
            if step < n_steps - 1:
                matmul(step % 2)
            if step > 0:
                epilogue(step - 1, (step - 1) % 2)


def inproj_qkv(a, w_in, q_norm, k_norm, cos_full, sin_signed, seq_len, tm=IN_PROJ_TM):
    t, d = a.shape
    tm = min(tm, seq_len)
    n_seq_tiles = seq_len // tm
    assert seq_len % tm == 0 and t % seq_len == 0 and tm % ATTN_TQ == 0 and tm % ATTN_TK == 0, (t, seq_len, tm)
    return pl.pallas_call(
        _inproj_qkv_kernel,
        grid=(t // tm, N_Q_TILES + 3),
        in_specs=[pl.BlockSpec((tm, d), lambda i, j: (i, 0)),
                  pl.BlockSpec((d, QKV_TN), lambda i, j: (0, jnp.minimum(j, N_Q_TILES + 1))),
                  pl.BlockSpec((1, HEAD_DIM), lambda i, j: (0, 0)),
                  pl.BlockSpec((1, HEAD_DIM), lambda i, j: (0, 0)),
                  pl.BlockSpec((tm, HEAD_DIM), lambda i, j: (i % n_seq_tiles, 0)),
                  pl.BlockSpec((tm, HEAD_DIM), lambda i, j: (i % n_seq_tiles, 0))],
        out_specs=[pl.BlockSpec((KV_GROUP, tm // ATTN_TQ, HEAD_DIM, ATTN_TQ),
                                lambda i, j: (jnp.clip(j - 1, 0, N_Q_TILES - 1), i, 0, 0)),
                   pl.BlockSpec((N_KV_HEADS, tm, HEAD_DIM), lambda i, j: (0, i, 0)),
                   pl.BlockSpec((N_KV_HEADS, tm // ATTN_TK, HEAD_DIM, ATTN_TK), lambda i, j: (0, i, 0, 0))],
        out_shape=[jax.ShapeDtypeStruct((N_Q_HEADS, t // ATTN_TQ, HEAD_DIM, ATTN_TQ), BF16),
                   jax.ShapeDtypeStruct((N_KV_HEADS, t, HEAD_DIM), BF16),
                   jax.ShapeDtypeStruct((N_KV_HEADS, t // ATTN_TK, HEAD_DIM, ATTN_TK), BF16)],
        scratch_shapes=[pltpu.VMEM((2, tm, QKV_TN), F32)],
        compiler_params=_cparams(("parallel", "arbitrary")),
        name="inproj_qkv",
    )(a, w_in, q_norm.reshape(1, HEAD_DIM), k_norm.reshape(1, HEAD_DIM), cos_full, sin_signed)


def _inproj_conv_kernel(a_ref, wb_ref, wc_ref, wh_ref, gb_ref, u_ref):
    a = a_ref[...]
    gb_ref[...] = jnp.dot(a, wb_ref[...], preferred_element_type=F32).astype(gb_ref.dtype)
    c = jnp.dot(a, wc_ref[...], preferred_element_type=F32)
    h = jnp.dot(a, wh_ref[...], preferred_element_type=F32)
    u_ref[...] = (c * h).astype(u_ref.dtype)


def inproj_conv(a, w_in, tm=IN_PROJ_TM, tn=QKV_TN):
    t, d = a.shape
    tm = min(tm, t)
    assert t % tm == 0 and CONV_WIDTH % tn == 0 and (ATTN_WIDTH + 2 * KV_WIDTH) % tn == 0, (t, tm, tn)
    off_b = (ATTN_WIDTH + 2 * KV_WIDTH) // tn
    off_c = off_b + CONV_WIDTH // tn
    off_h = off_c + CONV_WIDTH // tn
    out = jax.ShapeDtypeStruct((t, CONV_WIDTH), BF16)
    return pl.pallas_call(
        _inproj_conv_kernel,
        grid=(t // tm, CONV_WIDTH // tn),
        in_specs=[pl.BlockSpec((tm, d), lambda i, j: (i, 0)),
                  pl.BlockSpec((d, tn), lambda i, j: (0, off_b + j)),
                  pl.BlockSpec((d, tn), lambda i, j: (0, off_c + j)),
                  pl.BlockSpec((d, tn), lambda i, j: (0, off_h + j))],
        out_specs=[pl.BlockSpec((tm, tn), lambda i, j: (i, j)),
                   pl.BlockSpec((tm, tn), lambda i, j: (i, j))],
        out_shape=[out, out],
        compiler_params=_cparams(("parallel", "arbitrary")),
        name="inproj_conv",
    )(a, w_in, w_in, w_in)


def _attn_kernel(qt_ref, k_ref, vt_ref, o_ref, m_ref, l_ref, acc_ref, s_ref, p_ref, alpha_ref, *, shifted):
    group, n_q_tiles, _, tq = qt_ref.shape
    n_heads = group * n_q_tiles
    n_chunks = vt_ref.shape[1]
    tk = vt_ref.shape[3]
    slabs = [pl.ds(r, ATTN_SLAB) for r in range(0, tk, ATTN_SLAB)]

    def q_tile(h):
        return qt_ref[h % group, h // group]

    def key_chunk(c):
        return k_ref[0, pl.ds(pl.multiple_of(c * tk, tk), tk), :]

    l_ref[...] = jnp.zeros(l_ref.shape, F32)
    acc_ref[...] = jnp.zeros(acc_ref.shape, F32)
    p_ref[...] = jnp.zeros(p_ref.shape, p_ref.dtype)

    def pipeline(stage_scores, stage_values, unroll):
        unroll = unroll if n_chunks % unroll == 0 else 1

        @pl.loop(0, n_chunks // unroll)
        def _(step):
            for sub in range(unroll):
                c = step * unroll + sub
                for h in range(n_heads):
                    slot, other = h % 2, (h + 1) % 2
                    stage_scores(c, h, slot)
                    if h > 0:
                        stage_values(c, h - 1, other)
                    else:
                        stage_values(jnp.maximum(c - 1, 0), n_heads - 1, other)
        stage_values(n_chunks - 1, n_heads - 1, (n_heads - 1) % 2)

    def unshifted():
        def probs(c, h, slot):
            s = jnp.dot(key_chunk(c), q_tile(h), preferred_element_type=F32)
            slab_sum = jnp.zeros((ATTN_SLAB, s.shape[1]), F32)
            for r in range(0, tk, ATTN_SLAB):
                p = jnp.exp2(s[r:r + ATTN_SLAB])
                slab_sum = slab_sum + p
                p_ref[slot, pl.ds(r, ATTN_SLAB), :] = p.astype(p_ref.dtype)
            l_ref[h] = l_ref[h] + jnp.sum(slab_sum, axis=0, keepdims=True)

        def values(c, h, slot):
            acc_ref[h] = acc_ref[h] + jnp.dot(vt_ref[0, c], p_ref[slot], preferred_element_type=F32)

        pipeline(probs, values, ATTN_CHUNK_UNROLL)

    def running_max():
        def scores(c, h, slot):
            s_ref[slot] = jnp.dot(key_chunk(c), q_tile(h), preferred_element_type=F32)

        def softmax(h, slot):
            m_old = m_ref[h]
            slab_max = s_ref[slot, slabs[0], :]
            for rows in slabs[1:]:
                slab_max = jnp.maximum(slab_max, s_ref[slot, rows, :])
            m_new = jnp.maximum(m_old, jnp.max(slab_max, axis=0, keepdims=True))
            alpha = jnp.exp2(m_old - m_new)
            slab_sum = jnp.zeros(slab_max.shape, F32)
            for rows in slabs:
                p = jnp.exp2(s_ref[slot, rows, :] - m_new)
                slab_sum = slab_sum + p
                p_ref[slot, rows, :] = p.astype(p_ref.dtype)
            l_ref[h] = alpha * l_ref[h] + jnp.sum(slab_sum, axis=0, keepdims=True)
            m_ref[h] = m_new
            alpha_ref[slot] = alpha

        def scores_and_softmax(c, h, slot):
            if h + 1 < n_heads:
                scores(c, h + 1, (h + 1) % 2)
            else:
                scores(jnp.minimum(c + 1, n_chunks - 1), 0, (h + 1) % 2)
            softmax(h, slot)

        def values(c, h, slot):
            pv = jnp.dot(vt_ref[0, c], p_ref[slot], preferred_element_type=F32)
            acc_ref[h] = alpha_ref[slot] * acc_ref[h] + pv

        m_ref[...] = jnp.full(m_ref.shape, -jnp.inf, F32)
        alpha_ref[...] = jnp.ones(alpha_ref.shape, F32)
        scores(0, 0, 0)
        pipeline(scores_and_softmax, values, 1)

    if shifted:
        running_max()
    else:
        unshifted()

    for h in range(n_heads):
        out_t = acc_ref[h] / l_ref[h]
        head, tile = h % group, h // group
        o_ref[tile * tq:(tile + 1) * tq, head * HEAD_DIM:(head + 1) * HEAD_DIM] = out_t.T.astype(o_ref.dtype)


def score_bound_sq(q_norm, k_norm):
    bf16_slack = (1.0 + 2.0 ** -8) ** 4
    return (HEAD_DIM * Q_SCALE) ** 2 * jnp.max(q_norm * q_norm) * jnp.max(k_norm * k_norm) * bf16_slack


def attention(qt, k, vt, batch, seq_len, bound_sq):
    unshifted_ok = bound_sq <= SCORE_BOUND_LOG2 * SCORE_BOUND_LOG2
    return lax.cond(unshifted_ok,
                    functools.partial(_attention_call, batch=batch, seq_len=seq_len, shifted=False),
                    functools.partial(_attention_call, batch=batch, seq_len=seq_len, shifted=True),
                    qt, k, vt)


def _attention_call(qt, k, vt, *, batch, seq_len, shifted):
    t = k.shape[1]
    q_tiles = min(ATTN_Q_TILES, seq_len // ATTN_TQ)
    nq = seq_len // (ATTN_TQ * q_tiles)
    nk = seq_len // ATTN_TK
    n_units = KV_GROUP * q_tiles
    assert seq_len % (ATTN_TQ * q_tiles) == 0 and seq_len % ATTN_TK == 0 and n_units % 2 == 0, (seq_len, q_tiles)
    assert t == batch * seq_len and ATTN_TK % ATTN_SLAB == 0, (t, batch, seq_len)
    return pl.pallas_call(
        functools.partial(_attn_kernel, shifted=shifted),
        grid=(batch, N_KV_HEADS, nq),
        in_specs=[pl.BlockSpec((KV_GROUP, q_tiles, HEAD_DIM, ATTN_TQ), lambda b, g, i: (g, b * nq + i, 0, 0)),
                  pl.BlockSpec((1, seq_len, HEAD_DIM), lambda b, g, i: (g, b, 0)),
                  pl.BlockSpec((1, nk, HEAD_DIM, ATTN_TK), lambda b, g, i: (g, b, 0, 0))],
        out_specs=pl.BlockSpec((ATTN_TQ * q_tiles, KV_GROUP * HEAD_DIM), lambda b, g, i: (b * nq + i, g)),
        out_shape=jax.ShapeDtypeStruct((t, ATTN_WIDTH), BF16),
        scratch_shapes=[pltpu.VMEM((n_units, 1, ATTN_TQ), F32),
                        pltpu.VMEM((n_units, 1, ATTN_TQ), F32),
                        pltpu.VMEM((n_units, HEAD_DIM, ATTN_TQ), F32),
                        pltpu.VMEM((2, ATTN_TK, ATTN_TQ), F32),
                        pltpu.VMEM((2, ATTN_TK, ATTN_TQ), BF16),
                        pltpu.VMEM((2, 1, ATTN_TQ), F32)],
        compiler_params=_cparams(("parallel", "parallel", "arbitrary")),
        name="attention_shifted" if shifted else "attention",
    )(qt, k, vt)


def _premix_kernel(attn_ref, gb_ref, u_ref, up_ref, un_ref, cw_ref, ga_ref, gc_ref, o_ref, *, n_seq_tiles):
    i = pl.program_id(0)
    tm = u_ref.shape[0]
    o_ref[:, :ATTN_WIDTH] = (_rms(attn_ref[...].astype(F32)) * ga_ref[...]).astype(o_ref.dtype)

    u = u_ref[...].astype(F32)
    first = (i % n_seq_tiles) == 0
    last = (i % n_seq_tiles) == n_seq_tiles - 1
    prev_row = jnp.where(first, 0.0, up_ref[BF16_SUBLANES - 1:BF16_SUBLANES, :].astype(F32))
    next_row = jnp.where(last, 0.0, un_ref[0:1, :].astype(F32))
    row = lax.broadcasted_iota(jnp.int32, u.shape, 0)
    u_prev = jnp.where(row == 0, prev_row, pltpu.roll(u, 1, axis=0))
    u_next = jnp.where(row == tm - 1, next_row, pltpu.roll(u, tm - 1, axis=0))
    cw = cw_ref[...]
    conv = gb_ref[...].astype(F32) * (cw[0:1] * u_prev + cw[1:2] * u + cw[2:3] * u_next)
    o_ref[:, ATTN_WIDTH:] = (_rms(conv) * gc_ref[...]).astype(o_ref.dtype)


def premix(attn, gate_b, u, conv_w, g_attn, g_conv, seq_len, tm=ROW_TILE):
    t = attn.shape[0]
    tm = min(tm, seq_len)
    n_seq_tiles = seq_len // tm
    assert seq_len % tm == 0 and t % seq_len == 0 and tm % BF16_SUBLANES == 0, (t, seq_len, tm)
    halo = tm // BF16_SUBLANES
    n_halo = t // BF16_SUBLANES
    row = lambda w: pl.BlockSpec((tm, w), lambda i: (i, 0))
    vec = lambda w: pl.BlockSpec((1, w), lambda i: (0, 0))
    return pl.pallas_call(
        functools.partial(_premix_kernel, n_seq_tiles=n_seq_tiles),
        grid=(t // tm,),
        in_specs=[row(ATTN_WIDTH), row(CONV_WIDTH), row(CONV_WIDTH),
                  pl.BlockSpec((BF16_SUBLANES, CONV_WIDTH), lambda i: (jnp.maximum(i * halo - 1, 0), 0)),
                  pl.BlockSpec((BF16_SUBLANES, CONV_WIDTH),
                               lambda i: (jnp.minimum((i + 1) * halo, n_halo - 1), 0)),
                  pl.BlockSpec((3, CONV_WIDTH), lambda i: (0, 0)),
                  vec(ATTN_WIDTH), vec(CONV_WIDTH)],
        out_specs=row(ATTN_WIDTH + CONV_WIDTH),
        out_shape=jax.ShapeDtypeStruct((t, ATTN_WIDTH + CONV_WIDTH), BF16),
        compiler_params=_cparams(("parallel",)),
        name="premix",
    )(attn, gate_b, u, u, u, conv_w, g_attn.reshape(1, -1), g_conv.reshape(1, -1))


def _rope_tables(seq_len):
    rows = seq_len // GRID_W
    inv = 1.0 / (ROPE_THETA ** (jnp.arange(0, ROPE_AXIS_DIM, 2, dtype=F32) / ROPE_AXIS_DIM))
    row_ang = jnp.arange(rows, dtype=F32)[:, None] * inv
    col_ang = jnp.arange(GRID_W, dtype=F32)[:, None] * inv
    ang = jnp.concatenate([
        jnp.broadcast_to(row_ang[:, None, :], (rows, GRID_W, inv.shape[0])),
        jnp.broadcast_to(col_ang[None, :, :], (rows, GRID_W, inv.shape[0])),
    ], axis=-1).reshape(seq_len, ROPE_AXIS_DIM)
    cos, sin = jnp.cos(ang), jnp.sin(ang)
    return jnp.concatenate([cos, cos], axis=-1), jnp.concatenate([-sin, sin], axis=-1)


def _prep_ffn(w_gate, w_up, w_down):
    return w_gate, w_up, w_down.astype(BF16)


def _ffn(hn, x, w, g_post, g_next):
    wg, wu, wd = w
    h = gate_up(hn, wg, wu)
    y = matmul(h, wd, *DOWN_TILE)
    return resid_norm_rows(y, x, g_post, FFN_RES_SCALE, g_next)


def _trunk(x3d, p):
    b, s, d = x3d.shape
    x = x3d.reshape(b * s, d)
    cos_full, sin_signed = _rope_tables(s)
    hn = rms_norm_rows(x, p["ffn1_pre"])
    x, hn = _ffn(hn, x, p["ffn1"], p["ffn1_post"], p["mix_pre"])
    qt, k, vt = inproj_qkv(hn, p["w_in"], p["q_norm"], p["k_norm"], cos_full, sin_signed, s)
    gate_b, u = inproj_conv(hn, p["w_in"])
    attn = attention(qt, k, vt, b, s, score_bound_sq(p["q_norm"], p["k_norm"]))
    mix_in = premix(attn, gate_b, u, p["conv_w"], p["attn_out_norm"], p["conv_out_norm"], s)
    y = matmul(mix_in, p["w_out"], *OUT_PROJ_TILE)
    x, hn = resid_norm_rows(y, x, p["mix_post"], 1.0, p["ffn2_pre"])
    x, _ = _ffn(hn, x, p["ffn2"], p["ffn2_post"], None)
    return x.reshape(b, s, d)


def kernel(x_prompt, x_sample, ffn1_pre, ffn1_post, w1_gate, w1_up, w1_down, mix_pre, mix_post, w_in, q_norm, k_norm, conv_w, attn_out_norm, conv_out_norm, w_out, ffn2_pre, ffn2_post, w2_gate, w2_up, w2_down):
    p = {
        "ffn1_pre": ffn1_pre[0], "ffn1_post": ffn1_post[0],
        "ffn1": _prep_ffn(w1_gate[0], w1_up[0], w1_down[0]),
        "mix_pre": mix_pre[0], "mix_post": mix_post[0],
        "w_in": w_in[0].astype(BF16), "q_norm": q_norm[0], "k_norm": k_norm[0],
        "conv_w": conv_w[0], "attn_out_norm": attn_out_norm[0], "conv_out_norm": conv_out_norm[0],
        "w_out": w_out[0].astype(BF16),
        "ffn2_pre": ffn2_pre[0], "ffn2_post": ffn2_post[0],
        "ffn2": _prep_ffn(w2_gate[0], w2_up[0], w2_down[0]),
    }
    return _trunk(x_prompt, p), _trunk(x_sample, p)
```
